```python
import jax, jax.numpy as jnp
from jax import lax
import numpy as np

D_MODEL = 2048
BATCH = 4
SEQ = 4096
DEPTH = 4

HEAD_DIM = 128
A_HEADS = 4
B_HEADS = 4
DILATION_GROUPS = ((128, 1), (512, 4), (2048, 16))
N_DIL = 3
C_HEADS = 6
Q_LORA = 512
KV_LORA = 256
D_NOPE = 128
D_ROPE = 64
D_VC = 128
ROPE_THETA = 500000.0
PARTIAL_ROT = HEAD_DIM // 4
Q_BLOCK = 128
WIN_BLOCK = 128
N_BRANCH = 3
FORGET_BIAS = 4.0
N_EXPERTS = 32
N_GROUPS = 8
EXPERTS_PER_GROUP = N_EXPERTS // N_GROUPS
TOP_K = 2
D_EXPERT = 512
EXPERT_BLOCK = 128
ALPHA = (2 * DEPTH) ** 0.25
BETA = (8 * DEPTH) ** -0.25
LN_EPS = 1e-5
ADA_SCALE = 0.1
W_A = A_HEADS * HEAD_DIM
W_B = B_HEADS * HEAD_DIM
W_C = C_HEADS * D_VC
IN_SIZES = (W_A, W_A, W_A, A_HEADS, N_DIL * W_B, N_DIL * W_B, N_DIL * W_B,
            Q_LORA, KV_LORA, D_ROPE, N_BRANCH * D_MODEL)
N_IN = sum(IN_SIZES)

kernel_name = 'hybrid_fox_dilated_mla_grouped_moe'


def layer_norm(x, g, b):
    xf = x.astype(jnp.float32)
    mu = jnp.mean(xf, axis=-1, keepdims=True)
    var = jnp.mean(jnp.square(xf - mu), axis=-1, keepdims=True)
    return ((xf - mu) * lax.rsqrt(var + LN_EPS) * g + b).astype(x.dtype)


def rms_norm(x, g):
    xf = x.astype(jnp.float32)
    return (xf * lax.rsqrt(jnp.mean(xf * xf, axis=-1, keepdims=True) + LN_EPS) * g).astype(x.dtype)


def apply_rope(x, positions, rot_dim):
    half = rot_dim // 2
    inv_freq = 1.0 / (ROPE_THETA ** (jnp.arange(0, rot_dim, 2, dtype=jnp.float32) / rot_dim))
    ang = positions.astype(jnp.float32)[..., None] * inv_freq
    ang = ang.reshape(ang.shape[:2] + (1,) * (x.ndim - 3) + (half,))
    cos, sin = jnp.cos(ang), jnp.sin(ang)
    xr = x[..., :rot_dim].astype(jnp.float32)
    x1, x2 = xr[..., :half], xr[..., half:]
    rot = jnp.concatenate([x1 * cos - x2 * sin, x2 * cos + x1 * sin], axis=-1).astype(x.dtype)
    return jnp.concatenate([rot, x[..., rot_dim:]], axis=-1)


def causal_block_attention(q, k, v, log_decay):
    B, S, H, dq = q.shape
    dv = v.shape[-1]
    nb = S // Q_BLOCK
    scale = dq ** -0.5
    qh = q.transpose(0, 2, 1, 3)
    kh = k.transpose(0, 2, 1, 3)
    vh = v.transpose(0, 2, 1, 3)
    qb = qh.reshape(B, H, nb, Q_BLOCK, dq).transpose(2, 0, 1, 3, 4)
    kpos = jnp.arange(S)

    def block(args):
        qi, i = args
        s = jnp.einsum('bhqd,bhkd->bhqk', qi, kh).astype(jnp.float32) * scale
        if log_decay is not None:
            fq = lax.dynamic_slice_in_dim(log_decay, i * Q_BLOCK, Q_BLOCK, axis=2)
            s = s + fq[..., None] - log_decay[:, :, None, :]
        qpos = i * Q_BLOCK + jnp.arange(Q_BLOCK)
        s = jnp.where(kpos[None, :] <= qpos[:, None], s, -jnp.inf)
        p = jax.nn.softmax(s, axis=-1)
        return jnp.einsum('bhqk,bhkd->bhqd', p.astype(vh.dtype), vh)

    o = lax.map(block, (qb, jnp.arange(nb)))
    return o.transpose(1, 0, 3, 2, 4).reshape(B, S, H * dv)


def dilated_window_attention(q, k, v, dilation, n_steps):
    B, S, H, dh = q.shape
    L = S // dilation
    nb = -(-L // WIN_BLOCK)
    Lp = nb * WIN_BLOCK

    def to_sub(t, front):
        t = t.reshape(B, L, dilation, H, dh).transpose(0, 2, 3, 1, 4)
        return jnp.pad(t, ((0, 0), (0, 0), (0, 0), (front, Lp - L), (0, 0)))

    qs = to_sub(q, 0).reshape(B, dilation, H, nb, WIN_BLOCK, dh)

    def band(t):
        tb = to_sub(t, WIN_BLOCK).reshape(B, dilation, H, nb + 1, WIN_BLOCK, dh)
        return jnp.concatenate([tb[:, :, :, :-1], tb[:, :, :, 1:]], axis=4)

    kk, vv = band(k), band(v)
    s = jnp.einsum('bzhnqe,bzhnke->bzhnqk', qs, kk).astype(jnp.float32) * (dh ** -0.5)
    qsub = (jnp.arange(nb) * WIN_BLOCK)[:, None, None] + jnp.arange(WIN_BLOCK)[None, :, None]
    ksub = (jnp.arange(nb) * WIN_BLOCK - WIN_BLOCK)[:, None, None] + jnp.arange(2 * WIN_BLOCK)[None, None, :]
    dist = qsub - ksub
    valid = (dist >= 0) & (dist <= n_steps) & (ksub >= 0)
    s = jnp.where(valid, s, -jnp.inf)
    lse = jax.nn.logsumexp(s, axis=-1)
    p = jnp.exp(s - lse[..., None])
    o = jnp.einsum('bzhnqk,bzhnke->bzhnqe', p.astype(vv.dtype), vv)
    o = o.reshape(B, dilation, H, Lp, dh)[:, :, :, :L].transpose(0, 3, 1, 2, 4).reshape(B, S, H, dh)
    lse = lse.reshape(B, dilation, H, Lp)[..., :L].transpose(0, 3, 1, 2).reshape(B, S, H)
    return o, lse


def token_mixers(h, positions, w_in, b_in, g_cq, g_ckv, w_uq, w_ukv, w_pa, w_pb, w_pc, w_o):
    B, S, _ = h.shape
    z = h @ w_in + b_in
    aq, ak, av, af, bq, bk, bv, cq, ckv, ckr, gl = jnp.split(
        z, np.cumsum(IN_SIZES)[:-1].tolist(), axis=-1)

    log_f = jnp.cumsum(jax.nn.log_sigmoid(af.astype(jnp.float32)), axis=1).transpose(0, 2, 1)
    o_a = causal_block_attention(aq.reshape(B, S, A_HEADS, HEAD_DIM), ak.reshape(B, S, A_HEADS, HEAD_DIM),
                                 av.reshape(B, S, A_HEADS, HEAD_DIM), log_f)

    bq = apply_rope(bq.reshape(B, S, N_DIL, B_HEADS, HEAD_DIM), positions, PARTIAL_ROT)
    bk = apply_rope(bk.reshape(B, S, N_DIL, B_HEADS, HEAD_DIM), positions, PARTIAL_ROT)
    bv = bv.reshape(B, S, N_DIL, B_HEADS, HEAD_DIM)
    outs, lses = [], []
    for g, (window, dil) in enumerate(DILATION_GROUPS):
        o_g, lse_g = dilated_window_attention(bq[:, :, g], bk[:, :, g], bv[:, :, g], dil, window // dil)
        outs.append(o_g)
        lses.append(lse_g)
    wgt = jax.nn.softmax(jnp.stack(lses), axis=0)
    o_b = jnp.einsum('gbsh,gbshe->bshe', wgt.astype(bv.dtype), jnp.stack(outs)).reshape(B, S, W_B)

    q_c = (rms_norm(cq, g_cq) @ w_uq).reshape(B, S, C_HEADS, D_NOPE + D_ROPE)
    q_c = jnp.concatenate([q_c[..., :D_NOPE], apply_rope(q_c[..., D_NOPE:], positions, D_ROPE)], axis=-1)
    kv = (rms_norm(ckv, g_ckv) @ w_ukv).reshape(B, S, C_HEADS, D_NOPE + D_VC)
    k_rope = apply_rope(ckr[:, :, None, :], positions, D_ROPE)
    k_c = jnp.concatenate([kv[..., :D_NOPE], jnp.broadcast_to(k_rope, (B, S, C_HEADS, D_ROPE))], axis=-1)
    o_c = causal_block_attention(q_c, k_c, kv[..., D_NOPE:], None)

    g_a, g_b, g_c = jnp.split(jax.nn.sigmoid(gl), N_BRANCH, axis=-1)
    y = g_a * (o_a @ w_pa) + g_b * (o_b @ w_pb) + g_c * (o_c @ w_pc)
    return y @ w_o


def grouped_moe(h, w_router, b_router, w_gate, w_up, w_down):
    T, D = h.shape
    s = jax.nn.sigmoid(h.astype(jnp.float32) @ w_router.astype(jnp.float32))
    sel = s + b_router.astype(jnp.float32)
    gscore = lax.top_k(sel.reshape(T, N_GROUPS, EXPERTS_PER_GROUP), 2)[0].sum(-1)
    gbest = jnp.argmax(gscore, axis=-1)
    in_group = (jnp.arange(N_EXPERTS) // EXPERTS_PER_GROUP)[None, :] == gbest[:, None]
    _, idx = lax.top_k(jnp.where(in_group, sel, -jnp.inf), TOP_K)
    wts = jnp.take_along_axis(s, idx, axis=-1)
    wts = wts / jnp.sum(wts, axis=-1, keepdims=True)

    n_assign = T * TOP_K
    e_flat = idx.reshape(-1)
    tok = jnp.repeat(jnp.arange(T, dtype=jnp.int32), TOP_K)
    order = jnp.argsort(e_flat)
    e_s, tok_s, w_s = e_flat[order], tok[order], wts.reshape(-1)[order]
    counts = jnp.bincount(e_flat, length=N_EXPERTS)
    start = jnp.cumsum(counts) - counts
    padded = (counts + EXPERT_BLOCK - 1) // EXPERT_BLOCK * EXPERT_BLOCK
    pend = jnp.cumsum(padded)
    pstart = pend - padded
    dest = pstart[e_s] + jnp.arange(n_assign) - start[e_s]
    R = (n_assign + N_EXPERTS * (EXPERT_BLOCK - 1) + EXPERT_BLOCK - 1) // EXPERT_BLOCK * EXPERT_BLOCK
    n_blk = R // EXPERT_BLOCK
    row_tok = jnp.full((R,), T, dtype=jnp.int32).at[dest].set(tok_s)
    row_w = jnp.zeros((R,), jnp.float32).at[dest].set(w_s)
    blk_e = jnp.minimum(jnp.searchsorted(pend, jnp.arange(n_blk) * EXPERT_BLOCK, side='right'), N_EXPERTS - 1)
    h_pad = jnp.concatenate([h, jnp.zeros((1, D), h.dtype)], axis=0)
    xs = h_pad[row_tok].reshape(n_blk, EXPERT_BLOCK, D)

    def expert_block(args):
        xb, e = args
        return (jax.nn.silu(xb @ w_gate[e]) * (xb @ w_up[e])) @ w_down[e]

    ys = lax.map(expert_block, (xs, blk_e)).reshape(R, D)
    out = jnp.zeros((T + 1, D), h.dtype).at[row_tok].add((ys * row_w[:, None]).astype(h.dtype))
    return out[:T]


def setup_inputs(seed: int = 0) -> dict:
    key = jax.random.key(seed)
    ks = jax.random.split(key, 24)
    f32 = jnp.float32

    def nrm(k, shape, scale):
        return scale * jax.random.normal(k, shape, f32)

    x = nrm(ks[0], (BATCH, SEQ, D_MODEL), 1.0)
    c = nrm(ks[1], (BATCH, D_MODEL), 1.0)
    positions = (jax.random.randint(ks[2], (BATCH, 1), 0, 8192, dtype=jnp.int32)
                 + jnp.arange(SEQ, dtype=jnp.int32)[None, :])
    w_ada = nrm(ks[3], (DEPTH, D_MODEL, 6 * D_MODEL), ADA_SCALE * D_MODEL ** -0.5)
    b_ada = nrm(ks[4], (DEPTH, 6 * D_MODEL), 0.01)
    w_in = nrm(ks[5], (DEPTH, D_MODEL, N_IN), D_MODEL ** -0.5)
    f0 = 3 * W_A
    b_in = nrm(ks[6], (DEPTH, N_IN), 0.01).at[:, f0:f0 + A_HEADS].add(FORGET_BIAS)
    g_cq = 1.0 + nrm(ks[7], (DEPTH, Q_LORA), 0.01)
    g_ckv = 1.0 + nrm(ks[8], (DEPTH, KV_LORA), 0.01)
    w_uq = nrm(ks[9], (DEPTH, Q_LORA, C_HEADS * (D_NOPE + D_ROPE)), Q_LORA ** -0.5)
    w_ukv = nrm(ks[10], (DEPTH, KV_LORA, C_HEADS * (D_NOPE + D_VC)), KV_LORA ** -0.5)
    w_pa = nrm(ks[11], (DEPTH, W_A, D_MODEL), W_A ** -0.5)
    w_pb = nrm(ks[12], (DEPTH, W_B, D_MODEL), W_B ** -0.5)
    w_pc = nrm(ks[13], (DEPTH, W_C, D_MODEL), W_C ** -0.5)
    w_o = nrm(ks[14], (DEPTH, D_MODEL, D_MODEL), BETA * D_MODEL ** -0.5)
    ln1_g = 1.0 + nrm(ks[15], (DEPTH, D_MODEL), 0.01)
    ln1_b = nrm(ks[16], (DEPTH, D_MODEL), 0.01)
    w_router = nrm(ks[17], (D_MODEL, N_EXPERTS), D_MODEL ** -0.5)
    b_router = nrm(ks[18], (N_EXPERTS,), 0.01)
    w_gate = nrm(ks[19], (DEPTH, N_EXPERTS, D_MODEL, D_EXPERT), D_MODEL ** -0.5)
    w_up = nrm(ks[20], (DEPTH, N_EXPERTS, D_MODEL, D_EXPERT), D_MODEL ** -0.5)
    w_down = nrm(ks[21], (DEPTH, N_EXPERTS, D_EXPERT, D_MODEL), BETA * D_EXPERT ** -0.5)
    ln2_g = 1.0 + nrm(ks[22], (DEPTH, D_MODEL), 0.01)
    ln2_b = nrm(ks[23], (DEPTH, D_MODEL), 0.01)
    return {'x': x, 'c': c, 'positions': positions, 'w_ada': w_ada, 'b_ada': b_ada,
            'w_in': w_in, 'b_in': b_in, 'g_cq': g_cq, 'g_ckv': g_ckv, 'w_uq': w_uq,
            'w_ukv': w_ukv, 'w_pa': w_pa, 'w_pb': w_pb, 'w_pc': w_pc, 'w_o': w_o,
            'ln1_g': ln1_g, 'ln1_b': ln1_b, 'w_router': w_router, 'b_router': b_router,
            'w_gate': w_gate, 'w_up': w_up, 'w_down': w_down, 'ln2_g': ln2_g, 'ln2_b': ln2_b}


def reference(x, c, positions, w_ada, b_ada, w_in, b_in, g_cq, g_ckv, w_uq, w_ukv, w_pa, w_pb,
              w_pc, w_o, ln1_g, ln1_b, w_router, b_router, w_gate, w_up, w_down, ln2_g, ln2_b):
    B, S, D = x.shape
    for l in range(DEPTH):
        mod = (c @ w_ada[l] + b_ada[l]).reshape(B, 6, D)[:, :, None, :]
        sh1, sc1, gt1, sh2, sc2, gt2 = [mod[:, i] for i in range(6)]
        h = x * (1.0 + sc1) + sh1
        mix = token_mixers(h, positions, w_in[l], b_in[l], g_cq[l], g_ckv[l], w_uq[l], w_ukv[l],
                           w_pa[l], w_pb[l], w_pc[l], w_o[l])
        x = layer_norm(ALPHA * x + (1.0 + gt1) * mix, ln1_g[l], ln1_b[l])
        h = x * (1.0 + sc2) + sh2
        ff = grouped_moe(h.reshape(B * S, D), w_router, b_router, w_gate[l], w_up[l],
                         w_down[l]).reshape(B, S, D)
        x = layer_norm(ALPHA * x + (1.0 + gt2) * ff, ln2_g[l], ln2_b[l])
    return x
```

```python
import functools

import jax
import jax.numpy as jnp
import numpy as np
from jax import lax
from jax.experimental import pallas as pl
from jax.experimental.pallas import tpu as pltpu

F32 = jnp.float32
BF16 = jnp.bfloat16
HIGHEST = lax.Precision.HIGHEST

LANES = 128
SUBLANES = 8
VMEM_LIMIT = 56 * 1024 * 1024

HEAD_DIM = 128
A_HEADS = 4
B_HEADS = 4
DILATION_GROUPS = ((128, 1), (512, 4), (2048, 16))
N_DIL = 3
C_HEADS = 6
Q_LORA = 512
KV_LORA = 256
D_NOPE = 128
D_ROPE = 64
D_VC = 128
ROPE_THETA = 500000.0
PARTIAL_ROT = HEAD_DIM // 4
WIN_BLOCK = 128
N_EXPERTS = 32
N_GROUPS = 8
EXPERTS_PER_GROUP = N_EXPERTS // N_GROUPS
TOP_K = 2
LN_EPS = 1e-5
W_A = A_HEADS * HEAD_DIM
W_B = B_HEADS * HEAD_DIM
W_C = C_HEADS * D_VC

TN = 512
TILE_AQ = 12
TILE_CQ = 15
TILE_CKV = 16
TILE_B0 = 17
N_TILES = 26
Z_TILES = 17
AF_LANE0 = KV_LORA + LANES
ROW_TILE = 16
MOE_BLOCK = 256


def _cparams(sem):
    return pltpu.CompilerParams(dimension_semantics=sem, vmem_limit_bytes=VMEM_LIMIT)


def _ada_kernel(c_ref, w_ref, b_ref, o_ref):
    o_ref[0] = jnp.dot(c_ref[...], w_ref[0], precision=HIGHEST,
                       preferred_element_type=F32) + b_ref[0]


def ada_modulation(c_pad, w_ada, b_ada):
    depth, d, n = w_ada.shape
    tn = 1024
    return pl.pallas_call(
        _ada_kernel,
        grid=(depth, n // tn),
        in_specs=[pl.BlockSpec((SUBLANES, d), lambda l, j: (0, 0)),
                  pl.BlockSpec((1, d, tn), lambda l, j: (l, 0, j)),
                  pl.BlockSpec((1, 1, tn), lambda l, j: (l, 0, j))],
        out_specs=pl.BlockSpec((1, SUBLANES, tn), lambda l, j: (l, 0, j)),
        out_shape=jax.ShapeDtypeStruct((depth, SUBLANES, n), F32),
        compiler_params=_cparams(("arbitrary", "arbitrary")),
        name="ada_modulation",
    )(c_pad, w_ada, b_ada.reshape(depth, 1, n))


def _rope_table_kernel(pos_ref, invf_ref, mhi_ref, mlo_ref, c_ref, sp_ref, sm_ref):
    ang = pos_ref[...].astype(F32) * invf_ref[...]
    sin = jnp.sin(ang)
    c_ref[...] = jnp.cos(ang)
    sp_ref[...] = sin * mhi_ref[...]
    sm_ref[...] = -sin * mlo_ref[...]


def rope_tables(pos_col, rot_dim):
    t = pos_col.shape[0]
    half = rot_dim // 2
    inv_freq = 1.0 / (ROPE_THETA ** (jnp.arange(0, rot_dim, 2, dtype=F32) / rot_dim))
    lane = np.arange(LANES)
    invf = jnp.where(lane < rot_dim, jnp.tile(inv_freq, LANES // half), 0.0).astype(F32)[None, :]
    mhi = jnp.asarray(((lane >= half) & (lane < rot_dim)).astype(np.float32))[None, :]
    mlo = jnp.asarray((lane < half).astype(np.float32))[None, :]
    tt = min(t, 1024)
    row = pl.BlockSpec((1, LANES), lambda i: (0, 0))
    out = pl.BlockSpec((tt, LANES), lambda i: (i, 0))
    return pl.pallas_call(
        _rope_table_kernel,
        grid=(t // tt,),
        in_specs=[pl.BlockSpec((tt, 1), lambda i: (i, 0)), row, row, row],
        out_specs=[out, out, out],
        out_shape=[jax.ShapeDtypeStruct((t, LANES), F32)] * 3,
        compiler_params=_cparams(("arbitrary",)),
        name="rope_tables",
    )(pos_col, invf, mhi, mlo)


def _rope_slab(x, c, sp, sm, half):
    return x * c + pltpu.roll(x, half, 1) * sp + pltpu.roll(x, LANES - half, 1) * sm


def _in_proj_kernel(x_ref, mod_ref, w_ref, b_ref, rc_ref, rp_ref, rm_ref,
                    z_ref, zf_ref, zb0_ref, zb1_ref, zb2_ref, h_scr, de_scr, *, tm):
    j = pl.program_id(1)

    @pl.when(j == 0)
    def _():
        shift = mod_ref[0, 0:1, :]
        scale = mod_ref[0, 1:2, :]
        h_scr[...] = (x_ref[...] * (1.0 + scale) + shift).astype(BF16)

    acc = jnp.dot(h_scr[...], w_ref[0], preferred_element_type=F32) + b_ref[0]
    qk_scale = HEAD_DIM ** -0.5

    @pl.when(j == TILE_AQ)
    def _():
        z_ref[...] = (acc * qk_scale).astype(BF16)

    @pl.when((j < Z_TILES) & (j != TILE_AQ))
    def _():
        z_ref[...] = acc.astype(BF16)

    @pl.when(j == TILE_CKV)
    def _():
        zf_ref[...] = acc[:, AF_LANE0:AF_LANE0 + LANES]

    zb_refs = (zb0_ref, zb1_ref, zb2_ref)
    for part in range(3):
        for g, (_, dil) in enumerate(DILATION_GROUPS):
            @pl.when(j == TILE_B0 + part * N_DIL + g)
            def _(part=part, g=g, dil=dil):
                a = acc * qk_scale if part == 0 else acc
                if part < 2:
                    slabs = []
                    for s in range(TN // LANES):
                        slabs.append(_rope_slab(a[:, s * LANES:(s + 1) * LANES], rc_ref[...],
                                                rp_ref[...], rm_ref[...], PARTIAL_ROT // 2))
                    a = jnp.concatenate(slabs, axis=1)
                out = zb_refs[g]
                if dil == 1:
                    out[0, 0] = a.astype(BF16)
                else:
                    for s in range(TN // LANES):
                        de_scr[s] = a[:, s * LANES:(s + 1) * LANES]
                    for r in range(dil):
                        out[0, r] = jnp.concatenate(
                            [de_scr[s, pl.ds(r, tm // dil, stride=dil), :] for s in range(TN // LANES)],
                            axis=1).astype(BF16)


def in_projection(x2d, mod_l, w_in_p, b_in_p, tabs_b, batch, seq):
    t, d = x2d.shape
    tm = min(1024, seq)
    tiles_per_batch = seq // tm
    rc, rp, rm = tabs_b
    zb_shapes, zb_specs = [], []
    for g, (_, dil) in enumerate(DILATION_GROUPS):
        l_sub = seq // dil
        zb_shapes.append(jax.ShapeDtypeStruct((batch, dil, l_sub, 3 * TN), BF16))
        zb_specs.append(pl.BlockSpec(
            (1, dil, tm // dil, TN),
            lambda i, j, g=g: (i // tiles_per_batch, 0, i % tiles_per_batch,
                               (j >= TILE_B0 + N_DIL + g).astype(jnp.int32)
                               + (j >= TILE_B0 + 2 * N_DIL + g).astype(jnp.int32))))
    tab = pl.BlockSpec((tm, LANES), lambda i, j: (i, 0))
    return pl.pallas_call(
        functools.partial(_in_proj_kernel, tm=tm),
        grid=(t // tm, N_TILES),
        in_specs=[pl.BlockSpec((tm, d), lambda i, j: (i, 0)),
                  pl.BlockSpec((1, 6, d), lambda i, j: (i // tiles_per_batch, 0, 0)),
                  pl.BlockSpec((1, d, TN), lambda i, j: (j, 0, 0)),
                  pl.BlockSpec((1, 1, TN), lambda i, j: (j, 0, 0)),
                  tab, tab, tab],
        out_specs=[pl.BlockSpec((tm, TN), lambda i, j: (i, jnp.minimum(j, Z_TILES - 1))),
                   pl.BlockSpec((tm, LANES), lambda i, j: (i, 0))] + zb_specs,
        out_shape=[jax.ShapeDtypeStruct((t, Z_TILES * TN), BF16),
                   jax.ShapeDtypeStruct((t, LANES), F32)] + zb_shapes,
        scratch_shapes=[pltpu.VMEM((tm, d), BF16), pltpu.VMEM((TN // LANES, tm, LANES), F32)],
        compiler_params=_cparams(("arbitrary", "arbitrary")),
        name="in_projection",
    )(x2d, mod_l, w_in_p, b_in_p, rc, rp, rm)


def pack_in_weights(w_in_l, b_in_l):
    d = w_in_l.shape[0]
    sizes = (W_A, W_A, W_A, A_HEADS, N_DIL * W_B, N_DIL * W_B, N_DIL * W_B, Q_LORA, KV_LORA, D_ROPE,
             3 * d)
    offs = np.concatenate([[0], np.cumsum(sizes)])
    names = ("aq", "ak", "av", "af", "bq", "bk", "bv", "cq", "ckv", "ckr", "gl")
    seg = {n: (int(offs[i]), int(offs[i + 1])) for i, n in enumerate(names)}

    def cols(m, name, lo=0, hi=None):
        a, b = seg[name]
        hi = b - a if hi is None else hi
        return lax.slice_in_dim(m, a + lo, a + hi, axis=m.ndim - 1)

    def build(m):
        zeros = lambda n: jnp.zeros(m.shape[:-1] + (n,), m.dtype)
        parts = [cols(m, "gl"), cols(m, "aq"), cols(m, "ak"), cols(m, "av"), cols(m, "cq"),
                 cols(m, "ckv"), cols(m, "ckr"), zeros(LANES - D_ROPE), cols(m, "af"),
                 zeros(LANES - A_HEADS)]
        for name in ("bq", "bk", "bv"):
            for g in range(N_DIL):
                parts.append(cols(m, name, g * W_B, (g + 1) * W_B))
        return jnp.concatenate(parts, axis=-1)

    w = build(w_in_l).astype(BF16)
    w = w.reshape(d, N_TILES, TN).transpose(1, 0, 2)
    b = build(b_in_l[None, :]).reshape(N_TILES, 1, TN)
    return w, b


def _logf_kernel(zf_ref, lf_ref, carry):
    @pl.when(pl.program_id(1) == 0)
    def _():
        carry[...] = jnp.zeros_like(carry)

    x = zf_ref[...]
    ls = jnp.minimum(x, 0.0) - jnp.log1p(jnp.exp(-jnp.abs(x)))
    ts = x.shape[0]
    row = lax.broadcasted_iota(jnp.int32, (ts, ts), 0)
    col = lax.broadcasted_iota(jnp.int32, (ts, ts), 1)
    tri = (row >= col).astype(F32)
    cs = jnp.dot(tri, ls, precision=HIGHEST, preferred_element_type=F32) + carry[0:1, :]
    lf_ref[...] = cs
    carry[...] = jnp.broadcast_to(cs[ts - 1:ts, :], carry.shape)


def log_forget_cumsum(zf, batch, seq):
    ts = min(256, seq)
    nt = seq // ts
    return pl.pallas_call(
        _logf_kernel,
        grid=(batch, nt),
        in_specs=[pl.BlockSpec((ts, LANES), lambda b, i: (b * nt + i, 0))],
        out_specs=pl.BlockSpec((ts, LANES), lambda b, i: (b * nt + i, 0)),
        out_shape=jax.ShapeDtypeStruct(zf.shape, F32),
        scratch_shapes=[pltpu.VMEM((SUBLANES, LANES), F32)],
        compiler_params=_cparams(("arbitrary", "arbitrary")),
        name="log_forget_cumsum",
    )(zf)


def _flash_kernel(qt_ref, kt_ref, *refs, decay):
    if decay:
        q_ref, k_ref, v_ref, lfq_ref, lfk_ref, o_ref, m_scr, l_scr, acc_scr = refs
    else:
        q_ref, k_ref, v_ref, o_ref, m_scr, l_scr, acc_scr = refs
    p_id = pl.program_id(2)
    qi = qt_ref[p_id]
    ki = kt_ref[p_id]

    @pl.when(ki == 0)
    def _():
        m_scr[...] = jnp.full_like(m_scr, -jnp.inf)
        l_scr[...] = jnp.zeros_like(l_scr)
        acc_scr[...] = jnp.zeros_like(acc_scr)

    def step(masked):
        s = lax.dot_general(q_ref[...], k_ref[...], (((1,), (1,)), ((), ())),
                            preferred_element_type=F32)
        if decay:
            s = s + lfq_ref[0, 0] - lfk_ref[0, 0]
        if masked:
            row = lax.broadcasted_iota(jnp.int32, s.shape, 0)
            col = lax.broadcasted_iota(jnp.int32, s.shape, 1)
            s = jnp.where(col <= row, s, -jnp.inf)
        m_prev = m_scr[...]
        m_new = jnp.maximum(m_prev, jnp.max(s, axis=1, keepdims=True))
        alpha = jnp.exp(m_prev - m_new)
        p = jnp.exp(s - m_new)
        l_scr[...] = alpha * l_scr[...] + jnp.sum(p, axis=1, keepdims=True)
        acc_scr[...] = alpha * acc_scr[...] + jnp.dot(p.astype(BF16), v_ref[...],
                                                      preferred_element_type=F32)
        m_scr[...] = m_new

    @pl.when(ki < qi)
    def _():
        step(False)

    @pl.when(ki == qi)
    def _():
        step(True)
        o_ref[...] = (acc_scr[...] / l_scr[...]).astype(o_ref.dtype)


def flash_attention(q_arr, k_arr, v_arr, q_col0, k_col0, v_col0, dq, dv, heads, batch, seq,
                    decay=None):
    tq = min(512, seq)
    nq = seq // tq
    pairs = [(a, b) for a in range(nq) for b in range(a + 1)]
    qt = jnp.asarray([p[0] for p in pairs], jnp.int32)
    kt = jnp.asarray([p[1] for p in pairs], jnp.int32)
    in_specs = [
        pl.BlockSpec((tq, dq), lambda b, h, p, qt, kt: (b * nq + qt[p], q_col0 + h)),
        pl.BlockSpec((tq, dq), lambda b, h, p, qt, kt: (b * nq + kt[p], k_col0 + h)),
        pl.BlockSpec((tq, dv), lambda b, h, p, qt, kt: (b * nq + kt[p], v_col0 + h)),
    ]
    args = [q_arr, k_arr, v_arr]
    if decay is not None:
        in_specs += [pl.BlockSpec((1, 1, tq, 1), lambda b, h, p, qt, kt: (b, h, qt[p], 0)),
                     pl.BlockSpec((1, 1, 1, tq), lambda b, h, p, qt, kt: (b, h, 0, kt[p]))]
        args += list(decay)
    grid_spec = pltpu.PrefetchScalarGridSpec(
        num_scalar_prefetch=2,
        grid=(batch, heads, len(pairs)),
        in_specs=in_specs,
        out_specs=pl.BlockSpec((tq, dv), lambda b, h, p, qt, kt: (b * nq + qt[p], h)),
        scratch_shapes=[pltpu.VMEM((tq, 1), F32), pltpu.VMEM((tq, 1), F32),
                        pltpu.VMEM((tq, dv), F32)])
    return pl.pallas_call(
        functools.partial(_flash_kernel, decay=decay is not None),
        grid_spec=grid_spec,
        out_shape=jax.ShapeDtypeStruct((batch * seq, heads * dv), BF16),
        compiler_params=_cparams(("arbitrary", "arbitrary", "arbitrary")),
        name="flash_fox" if decay is not None else "flash_mla",
    )(qt, kt, *args)


def _dilated_kernel(q_ref, kc_ref, kp_ref, vc_ref, vp_ref, o_ref, lse_ref):
    n = pl.program_id(2)
    wb = WIN_BLOCK
    row = lax.broadcasted_iota(jnp.int32, (wb, wb), 0)
    col = lax.broadcasted_iota(jnp.int32, (wb, wb), 1)
    cur_ok = col <= row
    prev_ok = (col >= row) & (n > 0)
    dn = (((1,), (1,)), ((), ()))
    for h in range(B_HEADS):
        sl = slice(h * HEAD_DIM, (h + 1) * HEAD_DIM)
        q = q_ref[0, 0, :, sl]
        s_c = lax.dot_general(q, kc_ref[0, 0, :, sl], dn, preferred_element_type=F32)
        s_p = lax.dot_general(q, kp_ref[0, 0, :, sl], dn, preferred_element_type=F32)
        s_c = jnp.where(cur_ok, s_c, -jnp.inf)
        s_p = jnp.where(prev_ok, s_p, -jnp.inf)
        m = jnp.maximum(jnp.max(s_c, axis=1, keepdims=True), jnp.max(s_p, axis=1, keepdims=True))
        p_c = jnp.exp(s_c - m)
        p_p = jnp.exp(s_p - m)
        l = jnp.sum(p_c, axis=1, keepdims=True) + jnp.sum(p_p, axis=1, keepdims=True)
        o = (jnp.dot(p_c.astype(BF16), vc_ref[0, 0, :, sl], preferred_element_type=F32)
             + jnp.dot(p_p.astype(BF16), vp_ref[0, 0, :, sl], preferred_element_type=F32))
        o_ref[0, 0, :, sl] = (o / l).astype(BF16)
        lse_ref[0, 0, :, sl] = jnp.broadcast_to(m + jnp.log(l), (wb, HEAD_DIM))


def dilated_attention(zb, batch):
    _, dil, l_sub, _ = zb.shape
    nb = l_sub // WIN_BLOCK
    blk = (1, 1, WIN_BLOCK, W_B)
    cur = lambda c: pl.BlockSpec(blk, lambda b, r, n: (b, r, n, c))
    prev = lambda c: pl.BlockSpec(blk, lambda b, r, n: (b, r, jnp.maximum(n - 1, 0), c))
    out = pl.BlockSpec(blk, lambda b, r, n: (b, r, n, 0))
    return pl.pallas_call(
        _dilated_kernel,
        grid=(batch, dil, nb),
        in_specs=[cur(0), cur(1), prev(1), cur(2), prev(2)],
        out_specs=[out, out],
        out_shape=[jax.ShapeDtypeStruct((batch, dil, l_sub, W_B), BF16),
                   jax.ShapeDtypeStruct((batch, dil, l_sub, W_B), F32)],
        compiler_params=_cparams(("arbitrary", "arbitrary", "arbitrary")),
        name="dilated_attention",
    )(zb, zb, zb, zb, zb)


def _rms(x, g):
    return x * lax.rsqrt(jnp.mean(x * x, axis=-1, keepdims=True) + LN_EPS) * g


def _mla_up_kernel(cq_ref, ckv_ref, gq_ref, gkv_ref, wq_ref, wk_ref, wv_ref, rc_ref, rp_ref, rm_ref,
                   q_ref, k_ref, v_ref):
    scale = (D_NOPE + D_ROPE) ** -0.5
    half = D_ROPE // 2
    rc, rp, rm = rc_ref[...], rp_ref[...], rm_ref[...]
    cqn = _rms(cq_ref[...].astype(F32), gq_ref[...]).astype(BF16)
    q = jnp.dot(cqn, wq_ref[...], preferred_element_type=F32) * scale
    ckvr = ckv_ref[...].astype(F32)
    ckvn = _rms(ckvr[:, :KV_LORA], gkv_ref[...]).astype(BF16)
    k_rope = _rope_slab(ckvr[:, KV_LORA:KV_LORA + LANES], rc, rp, rm, half).astype(BF16)
    k_nope = jnp.dot(ckvn, wk_ref[...], preferred_element_type=F32)
    v_ref[...] = jnp.dot(ckvn, wv_ref[...], preferred_element_type=F32).astype(BF16)
    for h in range(C_HEADS):
        lo = h * 2 * LANES
        q_ref[:, lo:lo + LANES] = q[:, lo:lo + LANES].astype(BF16)
        q_ref[:, lo + LANES:lo + 2 * LANES] = _rope_slab(
            q[:, lo + LANES:lo + 2 * LANES], rc, rp, rm, half).astype(BF16)
        k_ref[:, lo:lo + LANES] = k_nope[:, h * LANES:(h + 1) * LANES].astype(BF16)
        k_ref[:, lo + LANES:lo + 2 * LANES] = k_rope


def mla_up_projection(z, g_cq, g_ckv, wq_p, wk_p, wv_p, tabs_c):
    t = z.shape[0]
    tm = min(512, t)
    rc, rp, rm = tabs_c
    full = lambda a: pl.BlockSpec(a.shape, lambda i: (0,) * a.ndim)
    tab = pl.BlockSpec((tm, LANES), lambda i: (i, 0))
    row = lambda w: pl.BlockSpec((tm, w), lambda i: (i, 0))
    return pl.pallas_call(
        _mla_up_kernel,
        grid=(t // tm,),
        in_specs=[pl.BlockSpec((tm, TN), lambda i: (i, TILE_CQ)),
                  pl.BlockSpec((tm, TN), lambda i: (i, TILE_CKV)),
                  full(g_cq), full(g_ckv), full(wq_p), full(wk_p), full(wv_p), tab, tab, tab],
        out_specs=[row(C_HEADS * 2 * LANES), row(C_HEADS * 2 * LANES), row(W_C)],
        out_shape=[jax.ShapeDtypeStruct((t, C_HEADS * 2 * LANES), BF16),
                   jax.ShapeDtypeStruct((t, C_HEADS * 2 * LANES), BF16),
                   jax.ShapeDtypeStruct((t, W_C), BF16)],
        compiler_params=_cparams(("arbitrary",)),
        name="mla_up_projection",
    )(z, z, g_cq, g_ckv, wq_p, wk_p, wv_p, rc, rp, rm)


def pack_mla_weights(w_uq_l, w_ukv_l):
    wq = w_uq_l.reshape(Q_LORA, C_HEADS, D_NOPE + D_ROPE)
    wq = jnp.pad(wq, ((0, 0), (0, 0), (0, 2 * LANES - D_NOPE - D_ROPE)))
    wkv = w_ukv_l.reshape(KV_LORA, C_HEADS, D_NOPE + D_VC)
    wk = wkv[:, :, :D_NOPE].reshape(KV_LORA, C_HEADS * D_NOPE)
    wv = wkv[:, :, D_NOPE:].reshape(KV_LORA, W_C)
    return (wq.reshape(Q_LORA, C_HEADS * 2 * LANES).astype(BF16), wk.astype(BF16), wv.astype(BF16))


def _merge_kernel(gla_ref, glb_ref, glc_ref, oa_ref, ob0_ref, ob1_ref, ob2_ref,
                  ls0_ref, ls1_ref, ls2_ref, oc_ref, wa_ref, wb_ref, wc_ref, y_ref,
                  o_scr, l_scr, *, tm):
    for g, (o_ref, ls_ref) in enumerate(((ob0_ref, ls0_ref), (ob1_ref, ls1_ref), (ob2_ref, ls2_ref))):
        dil = DILATION_GROUPS[g][1]
        for h in range(B_HEADS):
            sl = slice(h * HEAD_DIM, (h + 1) * HEAD_DIM)
            for r in range(dil):
                rows = pl.ds(r, tm // dil, stride=dil) if dil > 1 else slice(None)
                o_scr[g * B_HEADS + h, rows, :] = o_ref[0, r, :, sl].astype(F32)
                l_scr[g * B_HEADS + h, rows, :] = ls_ref[0, r, :, sl]
    slabs = []
    for h in range(B_HEADS):
        l0, l1, l2 = l_scr[h], l_scr[B_HEADS + h], l_scr[2 * B_HEADS + h]
        m = jnp.maximum(jnp.maximum(l0, l1), l2)
        e0, e1, e2 = jnp.exp(l0 - m), jnp.exp(l1 - m), jnp.exp(l2 - m)
        den = e0 + e1 + e2
        slabs.append((e0 / den) * o_scr[h] + (e1 / den) * o_scr[B_HEADS + h]
                     + (e2 / den) * o_scr[2 * B_HEADS + h])
    o_b = jnp.concatenate(slabs, axis=1).astype(BF16)
    pa = jnp.dot(oa_ref[...], wa_ref[...], preferred_element_type=F32)
    pb = jnp.dot(o_b, wb_ref[...], preferred_element_type=F32)
    pc = jnp.dot(oc_ref[...], wc_ref[...], preferred_element_type=F32)
    y = (jax.nn.sigmoid(gla_ref[...].astype(F32)) * pa
         + jax.nn.sigmoid(glb_ref[...].astype(F32)) * pb
         + jax.nn.sigmoid(glc_ref[...].astype(F32)) * pc)
    y_ref[...] = y.astype(BF16)


def branch_merge(z, o_a, obs, lses, o_c, wa, wb, wc, batch, seq):
    t = z.shape[0]
    d = wa.shape[1]
    tm = min(512, seq)
    tpb = seq // tm
    full = lambda a: pl.BlockSpec(a.shape, lambda i: (0,) * a.ndim)
    gl = lambda c: pl.BlockSpec((tm, d), lambda i: (i, c))
    res = lambda dil: pl.BlockSpec((1, dil, tm // dil, W_B), lambda i: (i // tpb, 0, i % tpb, 0))
    res_specs = [res(dil) for _, dil in DILATION_GROUPS]
    return pl.pallas_call(
        functools.partial(_merge_kernel, tm=tm),
        grid=(t // tm,),
        in_specs=[gl(0), gl(1), gl(2), pl.BlockSpec((tm, W_A), lambda i: (i, 0))]
        + res_specs + res_specs
        + [pl.BlockSpec((tm, W_C), lambda i: (i, 0)), full(wa), full(wb), full(wc)],
        out_specs=pl.BlockSpec((tm, d), lambda i: (i, 0)),
        out_shape=jax.ShapeDtypeStruct((t, d), BF16),
        scratch_shapes=[pltpu.VMEM((N_DIL * B_HEADS, tm, HEAD_DIM), F32),
                        pltpu.VMEM((N_DIL * B_HEADS, tm, HEAD_DIM), F32)],
        compiler_params=_cparams(("arbitrary",)),
        name="branch_merge",
    )(z, z, z, o_a, *obs, *lses, o_c, wa, wb, wc)


def _layer_norm(x, g, b):
    mu = jnp.mean(x, axis=-1, keepdims=True)
    xc = x - mu
    var = jnp.mean(xc * xc, axis=-1, keepdims=True)
    return xc * lax.rsqrt(var + LN_EPS) * g + b


def _lane_pick(x, lane, idx):
    return jnp.sum(jnp.where(lane == idx, x, 0.0), axis=1, keepdims=True)


def _out_router_kernel(y_ref, x_ref, mod_ref, wo_ref, lng_ref, lnb_ref, wr_ref, br_ref,
                       x1_ref, h2_ref, ri_ref, rf_ref, cnt_ref, carry, *, tm, alpha):
    @pl.when(pl.program_id(0) == 0)
    def _():
        carry[...] = jnp.zeros_like(carry)

    gate1 = mod_ref[0, 2:3, :]
    shift2 = mod_ref[0, 3:4, :]
    scale2 = mod_ref[0, 4:5, :]
    mix = jnp.dot(y_ref[...], wo_ref[...], preferred_element_type=F32)
    x1 = _layer_norm(alpha * x_ref[...] + (1.0 + gate1) * mix, lng_ref[...], lnb_ref[...])
    x1_ref[...] = x1
    h2 = x1 * (1.0 + scale2) + shift2
    d = h2.shape[1]
    for c in range(d // LANES):
        h2_ref[pl.ds(c, tm, stride=d // LANES), :] = h2[:, c * LANES:(c + 1) * LANES]

    s = jax.nn.sigmoid(jnp.dot(h2, wr_ref[...], precision=HIGHEST, preferred_element_type=F32))
    sel = s + br_ref[...]
    lane_i = lax.broadcasted_iota(jnp.int32, (tm, LANES), 1)
    lane = lane_i.astype(F32)
    sel_m = [sel] + [pltpu.roll(sel, LANES - N_GROUPS * i, 1) for i in range(1, EXPERTS_PER_GROUP)]
    s_m = [s] + [pltpu.roll(s, LANES - N_GROUPS * i, 1) for i in range(1, EXPERTS_PER_GROUP)]
    gscore = None
    for a in range(EXPERTS_PER_GROUP):
        for b in range(a + 1, EXPERTS_PER_GROUP):
            pair = sel_m[a] + sel_m[b]
            gscore = pair if gscore is None else jnp.maximum(gscore, pair)
    gscore = jnp.where(lane < N_GROUPS, gscore, -jnp.inf)
    gmax = jnp.max(gscore, axis=1, keepdims=True)
    gbest = jnp.min(jnp.where(gscore == gmax, lane, float(LANES)), axis=1, keepdims=True)
    v = [_lane_pick(sel_m[i], lane, gbest) for i in range(EXPERTS_PER_GROUP)]
    u = [_lane_pick(s_m[i], lane, gbest) for i in range(EXPERTS_PER_GROUP)]

    def first_argmax(vals):
        best = vals[0]
        for x in vals[1:]:
            best = jnp.maximum(best, x)
        idx = jnp.full_like(gbest, float(EXPERTS_PER_GROUP - 1))
        for i in range(EXPERTS_PER_GROUP - 2, -1, -1):
            idx = jnp.where(vals[i] == best, float(i), idx)
        return idx

    i1 = first_argmax(v)
    i2 = first_argmax([jnp.where(i1 == i, -jnp.inf, v[i]) for i in range(EXPERTS_PER_GROUP)])

    def member(vals, idx):
        out = vals[EXPERTS_PER_GROUP - 1]
        for i in range(EXPERTS_PER_GROUP - 2, -1, -1):
            out = jnp.where(idx == i, vals[i], out)
        return out

    u1, u2 = member(u, i1), member(u, i2)
    w1 = u1 / (u1 + u2)
    w2 = u2 / (u1 + u2)
    e1 = gbest * EXPERTS_PER_GROUP + i1
    e2 = gbest * EXPERTS_PER_GROUP + i2
    lane1 = i1 * N_GROUPS + gbest
    lane2 = i2 * N_GROUPS + gbest

    onehot = ((lane == lane1) | (lane == lane2))
    row = lax.broadcasted_iota(jnp.int32, (tm, tm), 0)
    col = lax.broadcasted_iota(jnp.int32, (tm, tm), 1)
    before = (col < row).astype(BF16)
    cnt = jnp.dot(before, onehot.astype(BF16), preferred_element_type=F32) + carry[0:1, :]
    r1 = _lane_pick(cnt, lane, lane1)
    r2 = _lane_pick(cnt, lane, lane2)
    new_carry = carry[0:1, :] + jnp.sum(onehot.astype(F32), axis=0, keepdims=True)
    carry[...] = jnp.broadcast_to(new_carry, carry.shape)
    cnt_ref[...] = jnp.broadcast_to(new_carry, cnt_ref.shape)

    ri = jnp.where(lane_i == 0, e1, jnp.where(lane_i == 1, e2, jnp.where(lane_i == 2, r1, r2)))
    ri_ref[...] = ri.astype(jnp.int32)
    rf_ref[...] = jnp.where(lane_i == 0, w1, jnp.where(lane_i == 1, w2, 0.0))


def out_proj_router(y, x2d, mod_l, w_o, ln_g, ln_b, wr_p, br_p, batch, seq, alpha):
    t, d = x2d.shape
    tm = min(256, seq)
    tpb = seq // tm
    full = lambda a: pl.BlockSpec(a.shape, lambda i: (0,) * a.ndim)
    tile = pl.BlockSpec((tm, d), lambda i: (i, 0))
    lanes = pl.BlockSpec((tm, LANES), lambda i: (i, 0))
    rt = d // LANES
    return pl.pallas_call(
        functools.partial(_out_router_kernel, tm=tm, alpha=alpha),
        grid=(t // tm,),
        in_specs=[tile, tile, pl.BlockSpec((1, 6, d), lambda i: (i // tpb, 0, 0)),
                  full(w_o), full(ln_g), full(ln_b), full(wr_p), full(br_p)],
        out_specs=[tile, pl.BlockSpec((tm * rt, LANES), lambda i: (i, 0)), lanes, lanes,
                   pl.BlockSpec((SUBLANES, LANES), lambda i: (0, 0))],
        out_shape=[jax.ShapeDtypeStruct((t, d), F32),
                   jax.ShapeDtypeStruct((t * rt, LANES), F32),
                   jax.ShapeDtypeStruct((t, LANES), jnp.int32),
                   jax.ShapeDtypeStruct((t, LANES), F32),
                   jax.ShapeDtypeStruct((SUBLANES, LANES), F32)],
        scratch_shapes=[pltpu.VMEM((SUBLANES, LANES), F32)],
        compiler_params=_cparams(("arbitrary",)),
        name="out_proj_router",
    )(y, x2d, mod_l, w_o, ln_g, ln_b, wr_p, br_p)


def _dispatch_kernel(dest_ref, h_ref, xs_in_ref, xs_ref, sem, *, tm, rt):
    del xs_in_ref

    def row_copy(a, t):
        return pltpu.make_async_copy(h_ref.at[pl.ds(t * rt, rt), :], xs_ref.at[dest_ref[0, 0, a]], sem)

    def issue(t, carry):
        row_copy(t, t).start()
        row_copy(tm + t, t).start()
        return carry

    def drain(t, carry):
        row_copy(t, t).wait()
        row_copy(tm + t, t).wait()
        return carry

    lax.fori_loop(0, tm, issue, 0)
    lax.fori_loop(0, tm, drain, 0)


def moe_dispatch(dest_tiles, h2t, xs_buf, tm):
    nt = dest_tiles.shape[0]
    rt = h2t.shape[0] // (nt * tm)
    return pl.pallas_call(
        functools.partial(_dispatch_kernel, tm=tm, rt=rt),
        grid=(nt,),
        in_specs=[pl.BlockSpec((1, 1, 2 * tm), lambda i: (i, 0, 0), memory_space=pltpu.SMEM),
                  pl.BlockSpec((tm * rt, LANES), lambda i: (i, 0)),
                  pl.BlockSpec(memory_space=pl.ANY)],
        out_specs=pl.BlockSpec(memory_space=pl.ANY),
        out_shape=jax.ShapeDtypeStruct(xs_buf.shape, xs_buf.dtype),
        scratch_shapes=[pltpu.SemaphoreType.DMA(())],
        input_output_aliases={2: 0},
        compiler_params=_cparams(("arbitrary",)),
        name="moe_dispatch",
    )(dest_tiles, h2t, xs_buf)


def _expert_kernel(be_ref, nu_ref, xs_ref, wg_ref, wu_ref, wd_ref, ys_ref, wg_s, wu_s, wd_s, *, rt):
    n = pl.program_id(0)

    @pl.when(n < nu_ref[0])
    def _():
        prev = be_ref[jnp.maximum(n - 1, 0)]

        @pl.when((n == 0) | (be_ref[n] != prev))
        def _():
            wg_s[...] = wg_ref[0, 0].astype(BF16)
            wu_s[...] = wu_ref[0, 0].astype(BF16)
            wd_s[...] = wd_ref[0, 0].astype(BF16)

        x = jnp.concatenate([xs_ref[pl.ds(c, MOE_BLOCK, stride=rt), :] for c in range(rt)],
                            axis=1).astype(BF16)
        gate = jnp.dot(x, wg_s[...], preferred_element_type=F32)
        up = jnp.dot(x, wu_s[...], preferred_element_type=F32)
        act = (gate * jax.nn.sigmoid(gate) * up).astype(BF16)
        y = jnp.dot(act, wd_s[...], preferred_element_type=F32)
        for c in range(rt):
            ys_ref[pl.ds(c, MOE_BLOCK, stride=rt), :] = y[:, c * LANES:(c + 1) * LANES]


def expert_ffn(blk_e, n_used, xs2d, w_gate, w_up, w_down, layer):
    depth, n_exp, d, de = w_gate.shape
    rt = d // LANES
    n_blk = xs2d.shape[0] // (MOE_BLOCK * rt)
    rows = lambda n, be, nu: (jnp.minimum(n, nu[0] - 1), 0)
    grid_spec = pltpu.PrefetchScalarGridSpec(
        num_scalar_prefetch=2,
        grid=(n_blk,),
        in_specs=[pl.BlockSpec((MOE_BLOCK * rt, LANES), rows),
                  pl.BlockSpec((1, 1, d, de), lambda n, be, nu: (layer, be[n], 0, 0)),
                  pl.BlockSpec((1, 1, d, de), lambda n, be, nu: (layer, be[n], 0, 0)),
                  pl.BlockSpec((1, 1, de, d), lambda n, be, nu: (layer, be[n], 0, 0))],
        out_specs=pl.BlockSpec((MOE_BLOCK * rt, LANES), rows),
        scratch_shapes=[pltpu.VMEM((d, de), BF16), pltpu.VMEM((d, de), BF16),
                        pltpu.VMEM((de, d), BF16)])
    return pl.pallas_call(
        functools.partial(_expert_kernel, rt=rt),
        grid_spec=grid_spec,
        out_shape=jax.ShapeDtypeStruct(xs2d.shape, F32),
        compiler_params=_cparams(("arbitrary",)),
        name="expert_ffn",
    )(blk_e, n_used, xs2d, w_gate, w_up, w_down)


def _combine_kernel(dest_ref, ys_ref, rf_ref, x_ref, mod_ref, lng_ref, lnb_ref, o_ref, buf, sem,
                    *, tm, rt, alpha):
    def row_copy(a):
        return pltpu.make_async_copy(ys_ref.at[dest_ref[0, 0, a]], buf.at[pl.ds(a * rt, rt), :], sem)

    def issue(a, carry):
        row_copy(a).start()
        return carry

    def drain(a, carry):
        row_copy(a).wait()
        return carry

    lax.fori_loop(0, 2 * tm, issue, 0)
    lax.fori_loop(0, 2 * tm, drain, 0)

    def rows(k):
        return jnp.concatenate(
            [buf[pl.ds(k * tm * rt + c, tm, stride=rt), :] for c in range(rt)], axis=1)

    ff = rf_ref[:, 0:1] * rows(0) + rf_ref[:, 1:2] * rows(1)
    gate2 = mod_ref[0, 5:6, :]
    o_ref[...] = _layer_norm(alpha * x_ref[...] + (1.0 + gate2) * ff, lng_ref[...], lnb_ref[...])


def moe_combine(dest_tiles, ys3d, rf, x1, mod_l, ln_g, ln_b, batch, seq, tm, alpha):
    t, d = x1.shape
    rt = d // LANES
    tpb = seq // tm
    full = lambda a: pl.BlockSpec(a.shape, lambda i: (0,) * a.ndim)
    tile = pl.BlockSpec((tm, d), lambda i: (i, 0))
    return pl.pallas_call(
        functools.partial(_combine_kernel, tm=tm, rt=rt, alpha=alpha),
        grid=(t // tm,),
        in_specs=[pl.BlockSpec((1, 1, 2 * tm), lambda i: (i, 0, 0), memory_space=pltpu.SMEM),
                  pl.BlockSpec(memory_space=pl.ANY),
                  pl.BlockSpec((tm, LANES), lambda i: (i, 0)),
                  tile, pl.BlockSpec((1, 6, d), lambda i: (i // tpb, 0, 0)), full(ln_g), full(ln_b)],
        out_specs=tile,
        out_shape=jax.ShapeDtypeStruct((t, d), F32),
        scratch_shapes=[pltpu.VMEM((2 * tm * rt, LANES), F32), pltpu.SemaphoreType.DMA(())],
        compiler_params=_cparams(("arbitrary",)),
        name="moe_combine",
    )(dest_tiles, ys3d, rf, x1, mod_l, ln_g, ln_b)


def _routing_tables(ri, cnt, n_rows, tm):
    t = ri.shape[0]
    counts = cnt[0, :N_EXPERTS].astype(jnp.int32)
    counts = counts.reshape(EXPERTS_PER_GROUP, N_GROUPS).T.reshape(N_EXPERTS)
    padded = (counts + MOE_BLOCK - 1) // MOE_BLOCK * MOE_BLOCK
    pend = jnp.cumsum(padded)
    pstart = pend - padded
    dest = pstart[ri[:, 0:2]] + ri[:, 2:4]
    dest_tiles = dest.reshape(t // tm, tm, 2).transpose(0, 2, 1).reshape(t // tm, 1, 2 * tm)
    n_blk = n_rows // MOE_BLOCK
    blk_e = jnp.minimum(jnp.searchsorted(pend, jnp.arange(n_blk) * MOE_BLOCK, side='right'),
                        N_EXPERTS - 1).astype(jnp.int32)
    n_used = (pend[-1:] // MOE_BLOCK).astype(jnp.int32)
    return dest_tiles.astype(jnp.int32), blk_e, n_used


def kernel(x, c, positions, w_ada, b_ada, w_in, b_in, g_cq, g_ckv, w_uq, w_ukv, w_pa, w_pb, w_pc, w_o,
           ln1_g, ln1_b, w_router, b_router, w_gate, w_up, w_down, ln2_g, ln2_b):
    batch, seq, d = x.shape
    depth = w_ada.shape[0]
    t = batch * seq
    alpha = (2 * depth) ** 0.25
    rt = d // LANES
    assert rt == ROW_TILE and seq % 2048 == 0 and d % TN == 0
    tm_moe = min(256, seq)

    c_pad = jnp.zeros((SUBLANES, d), F32).at[:batch].set(c)
    mod_all = ada_modulation(c_pad, w_ada, b_ada)[:, :batch].reshape(depth, batch, 6, d)
    pos_col = positions.reshape(t, 1)
    tabs_b = rope_tables(pos_col, PARTIAL_ROT)
    tabs_c = rope_tables(pos_col, D_ROPE)

    perm = np.arange(N_EXPERTS).reshape(N_GROUPS, EXPERTS_PER_GROUP).T.reshape(-1)
    wr_p = jnp.zeros((d, LANES), F32).at[:, :N_EXPERTS].set(w_router.astype(F32)[:, perm])
    br_p = jnp.zeros((1, LANES), F32).at[0, :N_EXPERTS].set(b_router.astype(F32)[perm])

    n_rows = -(-(t * TOP_K + N_EXPERTS * (MOE_BLOCK - 1)) // MOE_BLOCK) * MOE_BLOCK
    xs_buf = jnp.zeros((n_rows, rt, LANES), F32)

    x2d = x.reshape(t, d)
    for l in range(depth):
        w_in_p, b_in_p = pack_in_weights(w_in[l], b_in[l])
        z, zf, zb0, zb1, zb2 = in_projection(x2d, mod_all[l], w_in_p, b_in_p, tabs_b, batch, seq)

        lf = log_forget_cumsum(zf, batch, seq)[:, :A_HEADS].reshape(batch, seq, A_HEADS)
        lf = lf.transpose(0, 2, 1)
        o_a = flash_attention(z, z, z, TILE_AQ * 4, TILE_AQ * 4 + 4, TILE_AQ * 4 + 8, HEAD_DIM,
                              HEAD_DIM, A_HEADS, batch, seq,
                              decay=(lf[:, :, :, None], lf[:, :, None, :]))

        obs, lses = [], []
        for zb in (zb0, zb1, zb2):
            o_g, lse_g = dilated_attention(zb, batch)
            obs.append(o_g)
            lses.append(lse_g)

        wq_p, wk_p, wv_p = pack_mla_weights(w_uq[l], w_ukv[l])
        q_c, k_c, v_c = mla_up_projection(z, g_cq[l][None, :], g_ckv[l][None, :], wq_p, wk_p, wv_p,
                                          tabs_c)
        o_c = flash_attention(q_c, k_c, v_c, 0, 0, 0, 2 * LANES, D_VC, C_HEADS, batch, seq)

        y = branch_merge(z, o_a, obs, lses, o_c, w_pa[l].astype(BF16), w_pb[l].astype(BF16),
                         w_pc[l].astype(BF16), batch, seq)
        x1, h2t, ri, rf, cnt = out_proj_router(y, x2d, mod_all[l], w_o[l].astype(BF16),
                                               ln1_g[l][None, :], ln1_b[l][None, :], wr_p, br_p,
                                               batch, seq, alpha)

        dest_tiles, blk_e, n_used = _routing_tables(ri, cnt, n_rows, tm_moe)
        xs_buf = moe_dispatch(dest_tiles, h2t, xs_buf, tm_moe)
        ys = expert_ffn(blk_e, n_used, xs_buf.reshape(n_rows * rt, LANES), w_gate, w_up, w_down, l)
        x2d = moe_combine(dest_tiles, ys.reshape(n_rows, rt, LANES), rf, x1, mod_all[l],
                          ln2_g[l][None, :], ln2_b[l][None, :], batch, seq, tm_moe, alpha)
    return x2d.reshape(batch, seq, d)
```

```python
import functools

import jax
import jax.numpy as jnp
import numpy as np
from jax import lax
from jax.experimental import pallas as pl
from jax.experimental.pallas import tpu as pltpu

F32 = jnp.float32
BF16 = jnp.bfloat16
HIGHEST = lax.Precision.HIGHEST

LANES = 128
SUBLANES = 8
VMEM_LIMIT = 56 * 1024 * 1024

HEAD_DIM = 128
A_HEADS = 4
B_HEADS = 4
DILATION_GROUPS = ((128, 1), (512, 4), (2048, 16))
N_DIL = 3
C_HEADS = 6
Q_LORA = 512
KV_LORA = 256
D_NOPE = 128
D_ROPE = 64
D_VC = 128
ROPE_THETA = 500000.0
PARTIAL_ROT = HEAD_DIM // 4
WIN_BLOCK = 128
N_EXPERTS = 32
N_GROUPS = 8
EXPERTS_PER_GROUP = N_EXPERTS // N_GROUPS
TOP_K = 2
LN_EPS = 1e-5
LOG2E = 1.4426950408889634
W_A =A_HEADS * HEAD_DIM
W_B = B_HEADS * HEAD_DIM
W_C = C_HEADS * D_VC

TN = 512
TILE_AQ = 12
TILE_CQ = 15
TILE_CKV = 16
TILE_B0 = 17
N_TILES = 26
Z_TILES = 17
AF_LANE0 = KV_LORA + LANES
ROW_TILE = 16
MOE_BLOCK = 256


def _cparams(sem):
    return pltpu.CompilerParams(dimension_semantics=sem, vmem_limit_bytes=VMEM_LIMIT)


def _ada_kernel(c_ref, w_ref, b_ref, o_ref):
    o_ref[0] = jnp.dot(c_ref[...], w_ref[0], precision=HIGHEST,
                       preferred_element_type=F32) + b_ref[0]


def ada_modulation(c_pad, w_ada, b_ada):
    depth, d, n = w_ada.shape
    tn = 1024
    return pl.pallas_call(
        _ada_kernel,
        grid=(depth, n // tn),
        in_specs=[pl.BlockSpec((SUBLANES, d), lambda l, j: (0, 0)),
                  pl.BlockSpec((1, d, tn), lambda l, j: (l, 0, j)),
                  pl.BlockSpec((1, 1, tn), lambda l, j: (l, 0, j))],
        out_specs=pl.BlockSpec((1, SUBLANES, tn), lambda l, j: (l, 0, j)),
        out_shape=jax.ShapeDtypeStruct((depth, SUBLANES, n), F32),
        compiler_params=_cparams(("arbitrary", "arbitrary")),
        name="ada_modulation",
    )(c_pad, w_ada, b_ada.reshape(depth, 1, n))


def _rope_table_kernel(pos_ref, invf_ref, mhi_ref, mlo_ref, c_ref, sp_ref, sm_ref):
    ang = pos_ref[...].astype(F32) * invf_ref[...]
    sin = jnp.sin(ang)
    c_ref[...] = jnp.cos(ang)
    sp_ref[...] = sin * mhi_ref[...]
    sm_ref[...] = -sin * mlo_ref[...]


def rope_tables(pos_col, rot_dim):
    t = pos_col.shape[0]
    half = rot_dim // 2
    inv_freq = 1.0 / (ROPE_THETA ** (jnp.arange(0, rot_dim, 2, dtype=F32) / rot_dim))
    lane = np.arange(LANES)
    invf = jnp.where(lane < rot_dim, jnp.tile(inv_freq, LANES // half), 0.0).astype(F32)[None, :]
    mhi = jnp.asarray(((lane >= half) & (lane < rot_dim)).astype(np.float32))[None, :]
    mlo = jnp.asarray((lane < half).astype(np.float32))[None, :]
    tt = min(t, 1024)
    row = pl.BlockSpec((1, LANES), lambda i: (0, 0))
    out = pl.BlockSpec((tt, LANES), lambda i: (i, 0))
    return pl.pallas_call(
        _rope_table_kernel,
        grid=(t // tt,),
        in_specs=[pl.BlockSpec((tt, 1), lambda i: (i, 0)), row, row, row],
        out_specs=[out, out, out],
        out_shape=[jax.ShapeDtypeStruct((t, LANES), F32)] * 3,
        compiler_params=_cparams(("arbitrary",)),
        name="rope_tables",
    )(pos_col, invf, mhi, mlo)


def _rope_slab(x, c, sp, sm, half):
    return x * c + pltpu.roll(x, half, 1) * sp + pltpu.roll(x, LANES - half, 1) * sm


def _in_proj_kernel(x_ref, mod_ref, w_ref, b_ref, rc_ref, rp_ref, rm_ref,
                    z_ref, zf_ref, zb0_ref, zb1_ref, zb2_ref, h_scr, de_scr, *, tm):
    j = pl.program_id(1)

    @pl.when(j == 0)
    def _():
        shift = mod_ref[0, 0:1, :]
        scale = mod_ref[0, 1:2, :]
        h_scr[...] = (x_ref[...] * (1.0 + scale) + shift).astype(BF16)

    acc = jnp.dot(h_scr[...], w_ref[...], preferred_element_type=F32) + b_ref[...]
    qk_scale = HEAD_DIM ** -0.5

    @pl.when(j == TILE_AQ)
    def _():
        z_ref[...] = (acc * (qk_scale * LOG2E)).astype(BF16)

    @pl.when((j < Z_TILES) & (j != TILE_AQ))
    def _():
        z_ref[...] = acc.astype(BF16)

    @pl.when(j == TILE_CKV)
    def _():
        zf_ref[...] = acc[:, AF_LANE0:AF_LANE0 + LANES]

    zb_refs = (zb0_ref, zb1_ref, zb2_ref)
    for part in range(3):
        for g, (_, dil) in enumerate(DILATION_GROUPS):
            @pl.when(j == TILE_B0 + part * N_DIL + g)
            def _(part=part, g=g, dil=dil):
                a = acc * qk_scale if part == 0 else acc
                if part < 2:
                    slabs = []
                    for s in range(TN // LANES):
                        slabs.append(_rope_slab(a[:, s * LANES:(s + 1) * LANES], rc_ref[...],
                                                rp_ref[...], rm_ref[...], PARTIAL_ROT // 2))
                    a = jnp.concatenate(slabs, axis=1)
                out = zb_refs[g]
                if dil == 1:
                    out[0, 0] = a.astype(BF16)
                else:
                    for s in range(TN // LANES):
                        de_scr[s] = a[:, s * LANES:(s + 1) * LANES]
                    for r in range(dil):
                        out[0, r] = jnp.concatenate(
                            [de_scr[s, pl.ds(r, tm // dil, stride=dil), :] for s in range(TN // LANES)],
                            axis=1).astype(BF16)


def in_projection(x2d, mod_l, w_in_p, b_in_p, tabs_b, batch, seq):
    t, d = x2d.shape
    tm = min(1024, seq)
    tiles_per_batch = seq // tm
    rc, rp, rm = tabs_b
    zb_shapes, zb_specs = [], []
    for g, (_, dil) in enumerate(DILATION_GROUPS):
        l_sub = seq // dil
        zb_shapes.append(jax.ShapeDtypeStruct((batch, dil, l_sub, 3 * TN), BF16))
        zb_specs.append(pl.BlockSpec(
            (1, dil, tm // dil, TN),
            lambda i, j, g=g: (i // tiles_per_batch, 0, i % tiles_per_batch,
                               (j >= TILE_B0 + N_DIL + g).astype(jnp.int32)
                               + (j >= TILE_B0 + 2 * N_DIL + g).astype(jnp.int32))))
    tab = pl.BlockSpec((tm, LANES), lambda i, j: (i, 0))
    return pl.pallas_call(
        functools.partial(_in_proj_kernel, tm=tm),
        grid=(t // tm, N_TILES),
        in_specs=[pl.BlockSpec((tm, d), lambda i, j: (i, 0)),
                  pl.BlockSpec((1, 6, d), lambda i, j: (i // tiles_per_batch, 0, 0)),
                  pl.BlockSpec((d, TN), lambda i, j: (0, j)),
                  pl.BlockSpec((1, TN), lambda i, j: (0, j)),
                  tab, tab, tab],
        out_specs=[pl.BlockSpec((tm, TN), lambda i, j: (i, jnp.minimum(j, Z_TILES - 1))),
                   pl.BlockSpec((tm, LANES), lambda i, j: (i, 0))] + zb_specs,
        out_shape=[jax.ShapeDtypeStruct((t, Z_TILES * TN), BF16),
                   jax.ShapeDtypeStruct((t, LANES), F32)] + zb_shapes,
        scratch_shapes=[pltpu.VMEM((tm, d), BF16), pltpu.VMEM((TN // LANES, tm, LANES), F32)],
        compiler_params=_cparams(("arbitrary", "arbitrary")),
        name="in_projection",
    )(x2d, mod_l, w_in_p, b_in_p, rc, rp, rm)


def pack_in_weights(w_in_l, b_in_l):
    d = w_in_l.shape[0]
    o_af = 3 * W_A
    o_b = o_af + A_HEADS
    o_c = o_b + 3 * N_DIL * W_B
    o_gl = o_c + Q_LORA + KV_LORA + D_ROPE

    def build(m):
        cut = lambda a, b: lax.slice_in_dim(m, a, b, axis=1)
        zeros = lambda n: jnp.zeros((m.shape[0], n), m.dtype)
        return jnp.concatenate([cut(o_gl, o_gl + 3 * d), cut(0, o_af), cut(o_c, o_gl),
                                zeros(LANES - D_ROPE), cut(o_af, o_b), zeros(LANES - A_HEADS),
                                cut(o_b, o_c)], axis=1)

    return build(w_in_l.astype(BF16)), build(b_in_l[None, :])


def _logf_kernel(zf_ref, lf_ref, carry):
    @pl.when(pl.program_id(1) == 0)
    def _():
        carry[...] = jnp.zeros_like(carry)

    x = zf_ref[...]
    ls = jnp.minimum(x, 0.0) - jnp.log1p(jnp.exp(-jnp.abs(x)))
    ts = x.shape[0]
    row = lax.broadcasted_iota(jnp.int32, (ts, ts), 0)
    col = lax.broadcasted_iota(jnp.int32, (ts, ts), 1)
    tri = (row >= col).astype(F32)
    cs = jnp.dot(tri, ls, precision=HIGHEST, preferred_element_type=F32) + carry[0:1, :]
    lf_ref[...] = cs * LOG2E
    carry[...] = jnp.broadcast_to(cs[ts - 1:ts, :], carry.shape)


def log_forget_cumsum(zf, batch, seq):
    ts = min(256, seq)
    nt = seq // ts
    return pl.pallas_call(
        _logf_kernel,
        grid=(batch, nt),
        in_specs=[pl.BlockSpec((ts, LANES), lambda b, i: (b * nt + i, 0))],
        out_specs=pl.BlockSpec((ts, LANES), lambda b, i: (b * nt + i, 0)),
        out_shape=jax.ShapeDtypeStruct(zf.shape, F32),
        scratch_shapes=[pltpu.VMEM((SUBLANES, LANES), F32)],
        compiler_params=_cparams(("arbitrary", "arbitrary")),
        name="log_forget_cumsum",
    )(zf)


FLASH_TQ = 1024
FLASH_SUB = 512
V_PAD = 16


def _split3(x):
    hi = x.astype(BF16)
    r1 = x - hi.astype(F32)
    mid = r1.astype(BF16)
    lo = (r1 - mid.astype(F32)).astype(BF16)
    return hi, mid, lo


def _flash_kernel(*refs, decay, dv, seq):
    if decay:
        q_ref, k_ref, v_ref, lfq_ref, lfk_ref, o_ref, kext, vext, m_scr, acc_scr = refs
    else:
        q_ref, k_ref, v_ref, o_ref, vext, m_scr, acc_scr = refs
    qi = pl.program_id(2)
    tq = q_ref.shape[0]
    sub = min(FLASH_SUB, tq)
    n_sub = tq // sub
    dn = (((1,), (1,)), ((), ()))

    @pl.when(qi == 0)
    def _():
        lane = lax.broadcasted_iota(jnp.int32, (seq, LANES), 1)
        for c in range(seq // sub):
            cols = slice(c * sub, (c + 1) * sub)
            vext[0:dv, cols] = v_ref[cols, :].astype(F32).T.astype(BF16)
        ones_row = lax.broadcasted_iota(jnp.int32, (V_PAD, seq), 0) == 0
        vext[dv:, :] = jnp.where(ones_row, 1.0, 0.0).astype(BF16)
        if decay:
            hi, mid, lo = _split3(lfk_ref[0, 0])
            kext[:, :HEAD_DIM] = k_ref[...]
            kext[:, HEAD_DIM:] = jnp.where(
                lane == 0, -hi, jnp.where(lane == 1, -mid, jnp.where(
                    lane == 2, -lo, jnp.where(lane < 6, 1.0, 0.0).astype(BF16))))

    if decay:
        lane = lax.broadcasted_iota(jnp.int32, (tq, LANES), 1)
        hi, mid, lo = _split3(lfq_ref[0, 0])
        q_tail = jnp.where(lane == 3, hi, jnp.where(lane == 4, mid, jnp.where(
            lane == 5, lo, jnp.where(lane < 3, 1.0, 0.0).astype(BF16))))
        q_all = jnp.concatenate([q_ref[...], q_tail], axis=1)
        keys = kext
    else:
        q_all = q_ref[...]
        keys = k_ref

    m_scr[...] = jnp.full_like(m_scr, -jnp.inf)
    acc_scr[...] = jnp.zeros_like(acc_scr)

    def chain(r, c, masked):
        cols = slice(r * sub, (r + 1) * sub)
        start = pl.multiple_of(c * sub, sub)
        s = lax.dot_general(keys[pl.ds(start, sub), :], q_all[cols], dn, preferred_element_type=F32)
        if masked:
            key = lax.broadcasted_iota(jnp.int32, s.shape, 0)
            qry = lax.broadcasted_iota(jnp.int32, s.shape, 1)
            s = jnp.where(key <= qry, s, -jnp.inf)
        m_prev = m_scr[0:1, cols]
        m_new = jnp.maximum(m_prev, jnp.max(s, axis=0, keepdims=True))
        p = jnp.exp2(s - m_new).astype(BF16)
        acc_scr[:, cols] = (jnp.exp2(m_prev - m_new) * acc_scr[:, cols]
                            + jnp.dot(vext[:, pl.ds(start, sub)], p, preferred_element_type=F32))
        m_scr[:, cols] = jnp.broadcast_to(m_new, (SUBLANES, sub))

    def full_chunks(c, carry):
        for r in range(n_sub):
            chain(r, c, False)
        return carry

    lax.fori_loop(0, qi * n_sub, full_chunks, 0)
    for d in range(n_sub):
        c = qi * n_sub + d
        for r in range(d, n_sub):
            chain(r, c, r == d)
    for r in range(n_sub):
        cols = slice(r * sub, (r + 1) * sub)
        o_t = acc_scr[0:dv, cols] / acc_scr[dv:dv + 1, cols]
        o_ref[cols, :] = o_t.T.astype(o_ref.dtype)


def flash_attention(q_arr, k_arr, v_arr, q_col0, k_col0, v_col0, dq, dv, heads, batch, seq,
                    decay=None):
    tq = min(FLASH_TQ, seq)
    nq = seq // tq
    in_specs = [
        pl.BlockSpec((tq, dq), lambda b, h, i: (b * nq + i, q_col0 + h)),
        pl.BlockSpec((seq, dq), lambda b, h, i: (b, k_col0 + h)),
        pl.BlockSpec((seq, dv), lambda b, h, i: (b, v_col0 + h)),
    ]
    args = [q_arr, k_arr, v_arr]
    scratch = [pltpu.VMEM((dv + V_PAD, seq), BF16), pltpu.VMEM((SUBLANES, tq), F32),
               pltpu.VMEM((dv + V_PAD, tq), F32)]
    if decay is not None:
        in_specs += [pl.BlockSpec((1, 1, tq, 1), lambda b, h, i: (b, h, i, 0)),
                     pl.BlockSpec((1, 1, seq, 1), lambda b, h, i: (b, h, 0, 0))]
        args += [decay, decay]
        scratch = [pltpu.VMEM((seq, dq + LANES), BF16)] + scratch
    return pl.pallas_call(
        functools.partial(_flash_kernel, decay=decay is not None, dv=dv, seq=seq),
        grid=(batch, heads, nq),
        in_specs=in_specs,
        out_specs=pl.BlockSpec((tq, dv), lambda b, h, i: (b * nq + i, h)),
        out_shape=jax.ShapeDtypeStruct((batch * seq, heads * dv), BF16),
        scratch_shapes=scratch,
        compiler_params=_cparams(("arbitrary", "arbitrary", "arbitrary")),
        name="flash_fox" if decay is not None else "flash_mla",
    )(*args)


def _dilated_kernel(q_ref, kc_ref, kp_ref, vc_ref, vp_ref, o_ref, lse_ref):
    n = pl.program_id(2)
    wb = WIN_BLOCK
    row = lax.broadcasted_iota(jnp.int32, (wb, wb), 0)
    col = lax.broadcasted_iota(jnp.int32, (wb, wb), 1)
    cur_ok = col <= row
    prev_ok = (col >= row) & (n > 0)
    dn = (((1,), (1,)), ((), ()))
    for h in range(B_HEADS):
        sl = slice(h * HEAD_DIM, (h + 1) * HEAD_DIM)
        q = q_ref[0, 0, :, sl]
        s_c = lax.dot_general(q, kc_ref[0, 0, :, sl], dn, preferred_element_type=F32)
        s_p = lax.dot_general(q, kp_ref[0, 0, :, sl], dn, preferred_element_type=F32)
        s_c = jnp.where(cur_ok, s_c, -jnp.inf)
        s_p = jnp.where(prev_ok, s_p, -jnp.inf)
        m = jnp.maximum(jnp.max(s_c, axis=1, keepdims=True), jnp.max(s_p, axis=1, keepdims=True))
        p_c = jnp.exp(s_c - m)
        p_p = jnp.exp(s_p - m)
        l = jnp.sum(p_c, axis=1, keepdims=True) + jnp.sum(p_p, axis=1, keepdims=True)
        o = (jnp.dot(p_c.astype(BF16), vc_ref[0, 0, :, sl], preferred_element_type=F32)
             + jnp.dot(p_p.astype(BF16), vp_ref[0, 0, :, sl], preferred_element_type=F32))
        o_ref[0, 0, :, sl] = (o / l).astype(BF16)
        lse_ref[0, 0, :, sl] = jnp.broadcast_to(m + jnp.log(l), (wb, HEAD_DIM))


def dilated_attention(zb, batch):
    _, dil, l_sub, _ = zb.shape
    nb = l_sub // WIN_BLOCK
    blk = (1, 1, WIN_BLOCK, W_B)
    cur = lambda c: pl.BlockSpec(blk, lambda b, r, n: (b, r, n, c))
    prev = lambda c: pl.BlockSpec(blk, lambda b, r, n: (b, r, jnp.maximum(n - 1, 0), c))
    out = pl.BlockSpec(blk, lambda b, r, n: (b, r, n, 0))
    return pl.pallas_call(
        _dilated_kernel,
        grid=(batch, dil, nb),
        in_specs=[cur(0), cur(1), prev(1), cur(2), prev(2)],
        out_specs=[out, out],
        out_shape=[jax.ShapeDtypeStruct((batch, dil, l_sub, W_B), BF16),
                   jax.ShapeDtypeStruct((batch, dil, l_sub, W_B), F32)],
        compiler_params=_cparams(("arbitrary", "arbitrary", "arbitrary")),
        name="dilated_attention",
    )(zb, zb, zb, zb, zb)


def _rms(x, g):
    return x * lax.rsqrt(jnp.mean(x * x, axis=-1, keepdims=True) + LN_EPS) * g


def _mla_up_kernel(cq_ref, ckv_ref, gq_ref, gkv_ref, wq_ref, wk_ref, wv_ref, rc_ref, rp_ref, rm_ref,
                   q_ref, k_ref, v_ref):
    scale = (D_NOPE + D_ROPE) ** -0.5 * LOG2E
    half = D_ROPE // 2
    rc, rp, rm = rc_ref[...], rp_ref[...], rm_ref[...]
    cqn = _rms(cq_ref[...].astype(F32), gq_ref[...]).astype(BF16)
    q = jnp.dot(cqn, wq_ref[...], preferred_element_type=F32) * scale
    ckvr = ckv_ref[...].astype(F32)
    ckvn = _rms(ckvr[:, :KV_LORA], gkv_ref[...]).astype(BF16)
    k_rope = _rope_slab(ckvr[:, KV_LORA:KV_LORA + LANES], rc, rp, rm, half).astype(BF16)
    k_nope = jnp.dot(ckvn, wk_ref[...], preferred_element_type=F32)
    v_ref[...] = jnp.dot(ckvn, wv_ref[...], preferred_element_type=F32).astype(BF16)
    for h in range(C_HEADS):
        lo = h * 2 * LANES
        q_ref[:, lo:lo + LANES] = q[:, lo:lo + LANES].astype(BF16)
        q_ref[:, lo + LANES:lo + 2 * LANES] = _rope_slab(
            q[:, lo + LANES:lo + 2 * LANES], rc, rp, rm, half).astype(BF16)
        k_ref[:, lo:lo + LANES] = k_nope[:, h * LANES:(h + 1) * LANES].astype(BF16)
        k_ref[:, lo + LANES:lo + 2 * LANES] = k_rope


def mla_up_projection(z, g_cq, g_ckv, wq_p, wk_p, wv_p, tabs_c):
    t = z.shape[0]
    tm = min(512, t)
    rc, rp, rm = tabs_c
    full = lambda a: pl.BlockSpec(a.shape, lambda i: (0,) * a.ndim)
    tab = pl.BlockSpec((tm, LANES), lambda i: (i, 0))
    row = lambda w: pl.BlockSpec((tm, w), lambda i: (i, 0))
    return pl.pallas_call(
        _mla_up_kernel,
        grid=(t // tm,),
        in_specs=[pl.BlockSpec((tm, TN), lambda i: (i, TILE_CQ)),
                  pl.BlockSpec((tm, TN), lambda i: (i, TILE_CKV)),
                  full(g_cq), full(g_ckv), full(wq_p), full(wk_p), full(wv_p), tab, tab, tab],
        out_specs=[row(C_HEADS * 2 * LANES), row(C_HEADS * 2 * LANES), row(W_C)],
        out_shape=[jax.ShapeDtypeStruct((t, C_HEADS * 2 * LANES), BF16),
                   jax.ShapeDtypeStruct((t, C_HEADS * 2 * LANES), BF16),
                   jax.ShapeDtypeStruct((t, W_C), BF16)],
        compiler_params=_cparams(("arbitrary",)),
        name="mla_up_projection",
    )(z, z, g_cq, g_ckv, wq_p, wk_p, wv_p, rc, rp, rm)


def pack_mla_weights(w_uq_l, w_ukv_l):
    wq = w_uq_l.reshape(Q_LORA, C_HEADS, D_NOPE + D_ROPE)
    wq = jnp.pad(wq, ((0, 0), (0, 0), (0, 2 * LANES - D_NOPE - D_ROPE)))
    wkv = w_ukv_l.reshape(KV_LORA, C_HEADS, D_NOPE + D_VC)
    wk = wkv[:, :, :D_NOPE].reshape(KV_LORA, C_HEADS * D_NOPE)
    wv = wkv[:, :, D_NOPE:].reshape(KV_LORA, W_C)
    return (wq.reshape(Q_LORA, C_HEADS * 2 * LANES).astype(BF16), wk.astype(BF16), wv.astype(BF16))


def _merge_kernel(gla_ref, glb_ref, glc_ref, oa_ref, ob0_ref, ob1_ref, ob2_ref,
                  ls0_ref, ls1_ref, ls2_ref, oc_ref, wa_ref, wb_ref, wc_ref, y_ref,
                  o_scr, l_scr, *, tm):
    for g, (o_ref, ls_ref) in enumerate(((ob0_ref, ls0_ref), (ob1_ref, ls1_ref), (ob2_ref, ls2_ref))):
        dil = DILATION_GROUPS[g][1]
        for h in range(B_HEADS):
            sl = slice(h * HEAD_DIM, (h + 1) * HEAD_DIM)
            for r in range(dil):
                rows = pl.ds(r, tm // dil, stride=dil) if dil > 1 else slice(None)
                o_scr[g * B_HEADS + h, rows, :] = o_ref[0, r, :, sl].astype(F32)
                l_scr[g * B_HEADS + h, rows, :] = ls_ref[0, r, :, sl]
    slabs = []
    for h in range(B_HEADS):
        l0, l1, l2 = l_scr[h], l_scr[B_HEADS + h], l_scr[2 * B_HEADS + h]
        m = jnp.maximum(jnp.maximum(l0, l1), l2)
        e0, e1, e2 = jnp.exp(l0 - m), jnp.exp(l1 - m), jnp.exp(l2 - m)
        den = e0 + e1 + e2
        slabs.append((e0 / den) * o_scr[h] + (e1 / den) * o_scr[B_HEADS + h]
                     + (e2 / den) * o_scr[2 * B_HEADS + h])
    o_b = jnp.concatenate(slabs, axis=1).astype(BF16)
    pa = jnp.dot(oa_ref[...], wa_ref[...], preferred_element_type=F32)
    pb = jnp.dot(o_b, wb_ref[...], preferred_element_type=F32)
    pc = jnp.dot(oc_ref[...], wc_ref[...], preferred_element_type=F32)
    y = (jax.nn.sigmoid(gla_ref[...].astype(F32)) * pa
         + jax.nn.sigmoid(glb_ref[...].astype(F32)) * pb
         + jax.nn.sigmoid(glc_ref[...].astype(F32)) * pc)
    y_ref[...] = y.astype(BF16)


def branch_merge(z, o_a, obs, lses, o_c, wa, wb, wc, batch, seq):
    t = z.shape[0]
    d = wa.shape[1]
    tm = min(512, seq)
    tpb = seq // tm
    full = lambda a: pl.BlockSpec(a.shape, lambda i: (0,) * a.ndim)
    gl = lambda c: pl.BlockSpec((tm, d), lambda i: (i, c))
    res = lambda dil: pl.BlockSpec((1, dil, tm // dil, W_B), lambda i: (i // tpb, 0, i % tpb, 0))
    res_specs = [res(dil) for _, dil in DILATION_GROUPS]
    return pl.pallas_call(
        functools.partial(_merge_kernel, tm=tm),
        grid=(t // tm,),
        in_specs=[gl(0), gl(1), gl(2), pl.BlockSpec((tm, W_A), lambda i: (i, 0))]
        + res_specs + res_specs
        + [pl.BlockSpec((tm, W_C), lambda i: (i, 0)), full(wa), full(wb), full(wc)],
        out_specs=pl.BlockSpec((tm, d), lambda i: (i, 0)),
        out_shape=jax.ShapeDtypeStruct((t, d), BF16),
        scratch_shapes=[pltpu.VMEM((N_DIL * B_HEADS, tm, HEAD_DIM), F32),
                        pltpu.VMEM((N_DIL * B_HEADS, tm, HEAD_DIM), F32)],
        compiler_params=_cparams(("arbitrary",)),
        name="branch_merge",
    )(z, z, z, o_a, *obs, *lses, o_c, wa, wb, wc)


def _layer_norm(x, g, b):
    mu = jnp.mean(x, axis=-1, keepdims=True)
    xc = x - mu
    var = jnp.mean(xc * xc, axis=-1, keepdims=True)
    return xc * lax.rsqrt(var + LN_EPS) * g + b


def _out_router_kernel(y_ref, x_ref, mod_ref, wo_ref, lng_ref, lnb_ref, wr_ref, br_ref,
                       x1_ref, h2_ref, ri_ref, rf_ref, cnt_ref, carry, *, tm, alpha):
    @pl.when(pl.program_id(0) == 0)
    def _():
        carry[...] = jnp.zeros_like(carry)

    gate1 = mod_ref[0, 2:3, :]
    shift2 = mod_ref[0, 3:4, :]
    scale2 = mod_ref[0, 4:5, :]
    mix = jnp.dot(y_ref[...], wo_ref[...], preferred_element_type=F32)
    x1 = _layer_norm(alpha * x_ref[...] + (1.0 + gate1) * mix, lng_ref[...], lnb_ref[...])
    x1_ref[...] = x1
    h2 = x1 * (1.0 + scale2) + shift2
    d = h2.shape[1]
    for c in range(d // LANES):
        h2_ref[pl.ds(c, tm, stride=d // LANES), :] = h2[:, c * LANES:(c + 1) * LANES]

    h_hi = h2.astype(BF16)
    h_lo = (h2 - h_hi.astype(F32)).astype(BF16)
    logits = (jnp.dot(h_hi, wr_ref[0], preferred_element_type=F32)
              + jnp.dot(h_lo, wr_ref[0], preferred_element_type=F32)
              + jnp.dot(h_hi, wr_ref[1], preferred_element_type=F32))
    s = jax.nn.sigmoid(logits.T[:N_EXPERTS])
    sel = s + br_ref[...]
    sel_m = [sel[i * N_GROUPS:(i + 1) * N_GROUPS] for i in range(EXPERTS_PER_GROUP)]
    s_m = [s[i * N_GROUPS:(i + 1) * N_GROUPS] for i in range(EXPERTS_PER_GROUP)]
    gscore = None
    for a in range(EXPERTS_PER_GROUP):
        for b in range(a + 1, EXPERTS_PER_GROUP):
            pair = sel_m[a] + sel_m[b]
            gscore = pair if gscore is None else jnp.maximum(gscore, pair)
    grp = lax.broadcasted_iota(jnp.int32, (N_GROUPS, tm), 0).astype(F32)
    gmax = jnp.max(gscore, axis=0, keepdims=True)
    gbest = jnp.min(jnp.where(gscore == gmax, grp, float(N_GROUPS)), axis=0, keepdims=True)
    in_best = grp == gbest
    v = [jnp.sum(jnp.where(in_best, m, 0.0), axis=0, keepdims=True) for m in sel_m]
    u = [jnp.sum(jnp.where(in_best, m, 0.0), axis=0, keepdims=True) for m in s_m]

    def first_argmax(vals):
        best = vals[0]
        for x in vals[1:]:
            best = jnp.maximum(best, x)
        idx = jnp.full_like(gbest, float(EXPERTS_PER_GROUP - 1))
        for i in range(EXPERTS_PER_GROUP - 2, -1, -1):
            idx = jnp.where(vals[i] == best, float(i), idx)
        return idx

    i1 = first_argmax(v)
    i2 = first_argmax([jnp.where(i1 == i, -jnp.inf, v[i]) for i in range(EXPERTS_PER_GROUP)])

    def member(vals, idx):
        out = vals[EXPERTS_PER_GROUP - 1]
        for i in range(EXPERTS_PER_GROUP - 2, -1, -1):
            out = jnp.where(idx == i, vals[i], out)
        return out

    u1, u2 = member(u, i1), member(u, i2)
    w1 = u1 / (u1 + u2)
    w2 = u2 / (u1 + u2)
    e1 = gbest * EXPERTS_PER_GROUP + i1
    e2 = gbest * EXPERTS_PER_GROUP + i2
    row1 = i1 * N_GROUPS + gbest
    row2 = i2 * N_GROUPS + gbest

    erow = lax.broadcasted_iota(jnp.int32, (N_EXPERTS, tm), 0).astype(F32)
    hit1, hit2 = erow == row1, erow == row2
    onehot = (hit1 | hit2).astype(F32)
    t_row = lax.broadcasted_iota(jnp.int32, (tm, tm), 0)
    t_col = lax.broadcasted_iota(jnp.int32, (tm, tm), 1)
    earlier = (t_row < t_col).astype(BF16)
    cnt = jnp.dot(onehot.astype(BF16), earlier, preferred_element_type=F32) + carry[:, 0:1]
    r1 = jnp.sum(jnp.where(hit1, cnt, 0.0), axis=0, keepdims=True)
    r2 = jnp.sum(jnp.where(hit2, cnt, 0.0), axis=0, keepdims=True)
    new_carry = carry[...] + jnp.sum(onehot, axis=1, keepdims=True)
    carry[...] = new_carry
    cnt_ref[...] = new_carry

    sub = lax.broadcasted_iota(jnp.int32, (SUBLANES, tm), 0)
    ri = jnp.where(sub == 0, e1, jnp.where(sub == 1, e2, jnp.where(sub == 2, r1, jnp.where(
        sub == 3, r2, 0.0))))
    ri_ref[0] = ri.astype(jnp.int32)
    rf_ref[0] = jnp.where(sub == 0, w1, jnp.where(sub == 1, w2, 0.0))


def out_proj_router(y, x2d, mod_l, w_o, ln_g, ln_b, wr_p, br_p, batch, seq, alpha):
    t, d = x2d.shape
    tm = min(256, seq)
    tpb = seq // tm
    nt = t // tm
    full = lambda a: pl.BlockSpec(a.shape, lambda i: (0,) * a.ndim)
    tile = pl.BlockSpec((tm, d), lambda i: (i, 0))
    rows = pl.BlockSpec((1, SUBLANES, tm), lambda i: (i, 0, 0))
    rt = d // LANES
    return pl.pallas_call(
        functools.partial(_out_router_kernel, tm=tm, alpha=alpha),
        grid=(nt,),
        in_specs=[tile, tile, pl.BlockSpec((1, 6, d), lambda i: (i // tpb, 0, 0)),
                  full(w_o), full(ln_g), full(ln_b), full(wr_p), full(br_p)],
        out_specs=[tile, pl.BlockSpec((tm * rt, LANES), lambda i: (i, 0)), rows, rows,
                   pl.BlockSpec((N_EXPERTS, LANES), lambda i: (0, 0))],
        out_shape=[jax.ShapeDtypeStruct((t, d), F32),
                   jax.ShapeDtypeStruct((t * rt, LANES), F32),
                   jax.ShapeDtypeStruct((nt, SUBLANES, tm), jnp.int32),
                   jax.ShapeDtypeStruct((nt, SUBLANES, tm), F32),
                   jax.ShapeDtypeStruct((N_EXPERTS, LANES), F32)],
        scratch_shapes=[pltpu.VMEM((N_EXPERTS, LANES), F32)],
        compiler_params=_cparams(("arbitrary",)),
        name="out_proj_router",
    )(y, x2d, mod_l, w_o, ln_g, ln_b, wr_p, br_p)


def _dispatch_kernel(dest_ref, h_ref, xs_in_ref, xs_ref, sem, *, tm, rt):
    del xs_in_ref

    def row_copy(a, t):
        return pltpu.make_async_copy(h_ref.at[pl.ds(t * rt, rt), :], xs_ref.at[dest_ref[0, 0, a]], sem)

    def issue(t, carry):
        row_copy(t, t).start()
        row_copy(tm + t, t).start()
        return carry

    def drain(t, carry):
        row_copy(t, t).wait()
        row_copy(tm + t, t).wait()
        return carry

    lax.fori_loop(0, tm, issue, 0, unroll=8)
    lax.fori_loop(0, tm, drain, 0, unroll=8)


def moe_dispatch(dest_tiles, h2t, xs_buf, tm):
    nt = dest_tiles.shape[0]
    rt = h2t.shape[0] // (nt * tm)
    return pl.pallas_call(
        functools.partial(_dispatch_kernel, tm=tm, rt=rt),
        grid=(nt,),
        in_specs=[pl.BlockSpec((1, 1, 2 * tm), lambda i: (i, 0, 0), memory_space=pltpu.SMEM),
                  pl.BlockSpec((tm * rt, LANES), lambda i: (i, 0)),
                  pl.BlockSpec(memory_space=pl.ANY)],
        out_specs=pl.BlockSpec(memory_space=pl.ANY),
        out_shape=jax.ShapeDtypeStruct(xs_buf.shape, xs_buf.dtype),
        scratch_shapes=[pltpu.SemaphoreType.DMA(())],
        input_output_aliases={2: 0},
        compiler_params=_cparams(("arbitrary",)),
        name="moe_dispatch",
    )(dest_tiles, h2t, xs_buf)


def _expert_kernel(be_ref, nu_ref, xs_ref, wg_ref, wu_ref, wd_ref, ys_ref, wg_s, wu_s, wd_s, *, rt):
    n = pl.program_id(0)

    @pl.when(n < nu_ref[0])
    def _():
        prev = be_ref[jnp.maximum(n - 1, 0)]

        @pl.when((n == 0) | (be_ref[n] != prev))
        def _():
            wg_s[...] = wg_ref[0, 0].astype(BF16)
            wu_s[...] = wu_ref[0, 0].astype(BF16)
            wd_s[...] = wd_ref[0, 0].astype(BF16)

        x = jnp.concatenate([xs_ref[pl.ds(c, MOE_BLOCK, stride=rt), :] for c in range(rt)],
                            axis=1).astype(BF16)
        gate = jnp.dot(x, wg_s[...], preferred_element_type=F32)
        up = jnp.dot(x, wu_s[...], preferred_element_type=F32)
        act = (gate * jax.nn.sigmoid(gate) * up).astype(BF16)
        y = jnp.dot(act, wd_s[...], preferred_element_type=F32)
        for c in range(rt):
            ys_ref[pl.ds(c, MOE_BLOCK, stride=rt), :] = y[:, c * LANES:(c + 1) * LANES]

    @pl.when(n >= nu_ref[0])
    def _():
        ys_ref[...] = jnp.zeros_like(ys_ref)


def expert_ffn(blk_e, n_used, xs2d, w_gate, w_up, w_down, layer):
    depth, n_exp, d, de = w_gate.shape
    rt = d // LANES
    n_blk = xs2d.shape[0] // (MOE_BLOCK * rt)
    rows = lambda n, be, nu: (jnp.minimum(n, nu[0] - 1), 0)
    grid_spec = pltpu.PrefetchScalarGridSpec(
        num_scalar_prefetch=2,
        grid=(n_blk,),
        in_specs=[pl.BlockSpec((MOE_BLOCK * rt, LANES), rows),
                  pl.BlockSpec((1, 1, d, de), lambda n, be, nu: (layer, be[n], 0, 0)),
                  pl.BlockSpec((1, 1, d, de), lambda n, be, nu: (layer, be[n], 0, 0)),
                  pl.BlockSpec((1, 1, de, d), lambda n, be, nu: (layer, be[n], 0, 0))],
        out_specs=pl.BlockSpec((MOE_BLOCK * rt, LANES), lambda n, be, nu: (n, 0)),
        scratch_shapes=[pltpu.VMEM((d, de), BF16), pltpu.VMEM((d, de), BF16),
                        pltpu.VMEM((de, d), BF16)])
    return pl.pallas_call(
        functools.partial(_expert_kernel, rt=rt),
        grid_spec=grid_spec,
        out_shape=jax.ShapeDtypeStruct(xs2d.shape, F32),
        compiler_params=_cparams(("arbitrary",)),
        name="expert_ffn",
    )(blk_e, n_used, xs2d, w_gate, w_up, w_down)


def _combine_kernel(dest_ref, dnext_ref, ys_ref, rf_ref, x_ref, mod_ref, lng_ref, lnb_ref, o_ref,
                    buf, sem, *, tm, rt, alpha, nt):
    i = pl.program_id(0)
    slot = i % 2

    def row_copy(dref, a, sl):
        return pltpu.make_async_copy(ys_ref.at[dref[0, 0, a]], buf.at[sl, pl.ds(a * rt, rt), :],
                                     sem.at[sl])

    def issue(dref, sl):
        def body(a, carry):
            row_copy(dref, a, sl).start()
            return carry
        lax.fori_loop(0, 2 * tm, body, 0, unroll=8)

    @pl.when(i == 0)
    def _():
        issue(dest_ref, 0)

    @pl.when(i + 1 < nt)
    def _():
        issue(dnext_ref, 1 - slot)

    def drain(a, carry):
        row_copy(dest_ref, a, slot).wait()
        return carry

    lax.fori_loop(0, 2 * tm, drain, 0, unroll=8)
    cur = buf.at[slot]

    def rows(k):
        return jnp.concatenate(
            [cur[pl.ds(k * tm * rt + c, tm, stride=rt), :] for c in range(rt)], axis=1)

    ff = rf_ref[:, 0:1] * rows(0) + rf_ref[:, 1:2] * rows(1)
    gate2 = mod_ref[0, 5:6, :]
    o_ref[...] = _layer_norm(alpha * x_ref[...] + (1.0 + gate2) * ff, lng_ref[...], lnb_ref[...])


def moe_combine(dest_tiles, ys3d, rf, x1, mod_l, ln_g, ln_b, batch, seq, tm, alpha):
    t, d = x1.shape
    rt = d // LANES
    tpb = seq // tm
    full = lambda a: pl.BlockSpec(a.shape, lambda i: (0,) * a.ndim)
    tile = pl.BlockSpec((tm, d), lambda i: (i, 0))
    nt = t // tm
    return pl.pallas_call(
        functools.partial(_combine_kernel, tm=tm, rt=rt, alpha=alpha, nt=nt),
        grid=(nt,),
        in_specs=[pl.BlockSpec((1, 1, 2 * tm), lambda i: (i, 0, 0), memory_space=pltpu.SMEM),
                  pl.BlockSpec((1, 1, 2 * tm), lambda i: (jnp.minimum(i + 1, nt - 1), 0, 0),
                               memory_space=pltpu.SMEM),
                  pl.BlockSpec(memory_space=pl.ANY),
                  pl.BlockSpec((tm, TOP_K), lambda i: (i, 0)),
                  tile, pl.BlockSpec((1, 6, d), lambda i: (i // tpb, 0, 0)), full(ln_g), full(ln_b)],
        out_specs=tile,
        out_shape=jax.ShapeDtypeStruct((t, d), F32),
        scratch_shapes=[pltpu.VMEM((2, 2 * tm * rt, LANES), F32), pltpu.SemaphoreType.DMA((2,))],
        compiler_params=_cparams(("arbitrary",)),
        name="moe_combine",
    )(dest_tiles, dest_tiles, ys3d, rf, x1, mod_l, ln_g, ln_b)


def _routing_tables(ri, cnt, n_rows, tm):
    nt = ri.shape[0]
    counts = cnt[:, 0].astype(jnp.int32)
    counts = counts.reshape(EXPERTS_PER_GROUP, N_GROUPS).T.reshape(N_EXPERTS)
    padded = (counts + MOE_BLOCK - 1) // MOE_BLOCK * MOE_BLOCK
    pend = jnp.cumsum(padded)
    pstart = pend - padded
    dest = pstart[ri[:, 0:TOP_K, :]] + ri[:, TOP_K:2 * TOP_K, :]
    dest_tiles = dest.reshape(nt, 1, TOP_K * tm)
    n_blk = n_rows // MOE_BLOCK
    blk_start = jnp.arange(n_blk, dtype=jnp.int32) * MOE_BLOCK
    blk_e = jnp.minimum(jnp.sum(pend[None, :] <= blk_start[:, None], axis=1), N_EXPERTS - 1)
    n_used = (pend[-1:] // MOE_BLOCK).astype(jnp.int32)
    return dest_tiles.astype(jnp.int32), blk_e.astype(jnp.int32), n_used


def kernel(x, c, positions, w_ada, b_ada, w_in, b_in, g_cq, g_ckv, w_uq, w_ukv, w_pa, w_pb, w_pc, w_o,
           ln1_g, ln1_b, w_router, b_router, w_gate, w_up, w_down, ln2_g, ln2_b):
    batch, seq, d = x.shape
    depth = w_ada.shape[0]
    t = batch * seq
    alpha = (2 * depth) ** 0.25
    rt = d // LANES
    assert rt == ROW_TILE and seq % 2048 == 0 and d % TN == 0
    tm_moe = min(256, seq)

    c_pad = jnp.zeros((SUBLANES, d), F32).at[:batch].set(c)
    mod_all = ada_modulation(c_pad, w_ada, b_ada)[:, :batch].reshape(depth, batch, 6, d)
    pos_col = positions.reshape(t, 1)
    tabs_b = rope_tables(pos_col, PARTIAL_ROT)
    tabs_c = rope_tables(pos_col, D_ROPE)

    perm = np.arange(N_EXPERTS).reshape(N_GROUPS, EXPERTS_PER_GROUP).T.reshape(-1)
    wr_f = jnp.zeros((d, LANES), F32).at[:, :N_EXPERTS].set(w_router.astype(F32)[:, perm])
    wr_hi = wr_f.astype(BF16)
    wr_p = jnp.stack([wr_hi, (wr_f - wr_hi.astype(F32)).astype(BF16)])
    br_p = b_router.astype(F32)[perm][:, None]

    n_rows = -(-(t * TOP_K + N_EXPERTS * (MOE_BLOCK - 1)) // MOE_BLOCK) * MOE_BLOCK
    xs_buf = jnp.zeros((n_rows, rt, LANES), F32)

    x2d = x.reshape(t, d)
    for l in range(depth):
        w_in_p, b_in_p = pack_in_weights(w_in[l], b_in[l])
        z, zf, zb0, zb1, zb2 = in_projection(x2d, mod_all[l], w_in_p, b_in_p, tabs_b, batch, seq)

        lf = log_forget_cumsum(zf, batch, seq)[:, :A_HEADS].reshape(batch, seq, A_HEADS)
        lf = lf.transpose(0, 2, 1)[:, :, :, None]
        o_a = flash_attention(z, z, z, TILE_AQ * 4, TILE_AQ * 4 + 4, TILE_AQ * 4 + 8, HEAD_DIM,
                              HEAD_DIM, A_HEADS, batch, seq, decay=lf)

        obs, lses = [], []
        for zb in (zb0, zb1, zb2):
            o_g, lse_g = dilated_attention(zb, batch)
            obs.append(o_g)
            lses.append(lse_g)

        wq_p, wk_p, wv_p = pack_mla_weights(w_uq[l], w_ukv[l])
        q_c, k_c, v_c = mla_up_projection(z, g_cq[l][None, :], g_ckv[l][None, :], wq_p, wk_p, wv_p,
                                          tabs_c)
        o_c = flash_attention(q_c, k_c, v_c, 0, 0, 0, 2 * LANES, D_VC, C_HEADS, batch, seq)

        y = branch_merge(z, o_a, obs, lses, o_c, w_pa[l].astype(BF16), w_pb[l].astype(BF16),
                         w_pc[l].astype(BF16), batch, seq)
        x1, h2t, ri, rf, cnt = out_proj_router(y, x2d, mod_all[l], w_o[l].astype(BF16),
                                               ln1_g[l][None, :], ln1_b[l][None, :], wr_p, br_p,
                                               batch, seq, alpha)

        dest_tiles, blk_e, n_used = _routing_tables(ri, cnt, n_rows, tm_moe)
        wts = rf[:, 0:TOP_K, :].transpose(0, 2, 1).reshape(t, TOP_K)
        xs_buf = moe_dispatch(dest_tiles, h2t, xs_buf, tm_moe)
        ys = expert_ffn(blk_e, n_used, xs_buf.reshape(n_rows * rt, LANES), w_gate, w_up, w_down, l)
        x2d = moe_combine(dest_tiles, ys.reshape(n_rows, rt, LANES), wts, x1, mod_all[l],
                          ln2_g[l][None, :], ln2_b[l][None, :], batch, seq, tm_moe, alpha)
    return x2d.reshape(batch, seq, d)
```

```python
import functools

import jax
import jax.numpy as jnp
import numpy as np
from jax import lax
from jax.experimental import pallas as pl
from jax.experimental.pallas import tpu as pltpu

F32 = jnp.float32
BF16 = jnp.bfloat16
HIGHEST = lax.Precision.HIGHEST

LANES = 128
SUBLANES = 8
VMEM_LIMIT = 56 * 1024 * 1024

HEAD_DIM = 128
A_HEADS = 4
B_HEADS = 4
DILATION_GROUPS = ((128, 1), (512, 4), (2048, 16))
N_DIL = 3
C_HEADS = 6
Q_LORA = 512
KV_LORA = 256
D_NOPE = 128
D_ROPE = 64
D_VC = 128
ROPE_THETA = 500000.0
PARTIAL_ROT = HEAD_DIM // 4
WIN_BLOCK = 128
N_EXPERTS = 32
N_GROUPS = 8
EXPERTS_PER_GROUP = N_EXPERTS // N_GROUPS
TOP_K = 2
LN_EPS = 1e-5
LOG2E = 1.4426950408889634
W_A =A_HEADS * HEAD_DIM
W_B = B_HEADS * HEAD_DIM
W_C = C_HEADS * D_VC

TN = 512
TILE_AQ = 12
TILE_CQ = 15
TILE_CKV = 16
TILE_B0 = 17
N_TILES = 26
Z_TILES = 17
AF_LANE0 = KV_LORA + LANES
ROW_TILE = 16
MOE_BLOCK = 512


def _cparams(sem):
    return pltpu.CompilerParams(dimension_semantics=sem, vmem_limit_bytes=VMEM_LIMIT)


def _ada_kernel(c_ref, w_ref, b_ref, o_ref):
    o_ref[0] = jnp.dot(c_ref[...], w_ref[0], precision=HIGHEST,
                       preferred_element_type=F32) + b_ref[0]


def ada_modulation(c_pad, w_ada, b_ada):
    depth, d, n = w_ada.shape
    tn = 1024
    return pl.pallas_call(
        _ada_kernel,
        grid=(depth, n // tn),
        in_specs=[pl.BlockSpec((SUBLANES, d), lambda l, j: (0, 0)),
                  pl.BlockSpec((1, d, tn), lambda l, j: (l, 0, j)),
                  pl.BlockSpec((1, 1, tn), lambda l, j: (l, 0, j))],
        out_specs=pl.BlockSpec((1, SUBLANES, tn), lambda l, j: (l, 0, j)),
        out_shape=jax.ShapeDtypeStruct((depth, SUBLANES, n), F32),
        compiler_params=_cparams(("arbitrary", "arbitrary")),
        name="ada_modulation",
    )(c_pad, w_ada, b_ada.reshape(depth, 1, n))


def _rope_table_kernel(pos_ref, invf_ref, mhi_ref, mlo_ref, c_ref, sp_ref, sm_ref):
    ang = pos_ref[...].astype(F32) * invf_ref[...]
    sin = jnp.sin(ang)
    c_ref[...] = jnp.cos(ang)
    sp_ref[...] = sin * mhi_ref[...]
    sm_ref[...] = -sin * mlo_ref[...]


def rope_tables(pos_col, rot_dim):
    t = pos_col.shape[0]
    half = rot_dim // 2
    inv_freq = 1.0 / (ROPE_THETA ** (jnp.arange(0, rot_dim, 2, dtype=F32) / rot_dim))
    lane = np.arange(LANES)
    invf = jnp.where(lane < rot_dim, jnp.tile(inv_freq, LANES // half), 0.0).astype(F32)[None, :]
    mhi = jnp.asarray(((lane >= half) & (lane < rot_dim)).astype(np.float32))[None, :]
    mlo = jnp.asarray((lane < half).astype(np.float32))[None, :]
    tt = min(t, 1024)
    row = pl.BlockSpec((1, LANES), lambda i: (0, 0))
    out = pl.BlockSpec((tt, LANES), lambda i: (i, 0))
    return pl.pallas_call(
        _rope_table_kernel,
        grid=(t // tt,),
        in_specs=[pl.BlockSpec((tt, 1), lambda i: (i, 0)), row, row, row],
        out_specs=[out, out, out],
        out_shape=[jax.ShapeDtypeStruct((t, LANES), F32)] * 3,
        compiler_params=_cparams(("arbitrary",)),
        name="rope_tables",
    )(pos_col, invf, mhi, mlo)


def _rope_slab(x, c, sp, sm, half):
    return x * c + pltpu.roll(x, half, 1) * sp + pltpu.roll(x, LANES - half, 1) * sm


def _in_proj_kernel(x_ref, mod_ref, w_ref, b_ref, cs_ref, z_ref, zf_ref, h_ref):
    j = pl.program_id(1)

    @pl.when(j == 0)
    def _():
        shift = mod_ref[0, 0:1, :]
        scale = mod_ref[0, 1:2, :]
        h_ref[...] = (x_ref[...] * (1.0 + scale) + shift).astype(BF16)

    acc = (jnp.dot(h_ref[...], w_ref[...], preferred_element_type=F32) + b_ref[...]) * cs_ref[...]
    z_ref[...] = acc.astype(BF16)

    @pl.when(j == TILE_CKV)
    def _():
        zf_ref[...] = acc[:, AF_LANE0:AF_LANE0 + LANES]


def in_projection(x2d, mod_l, w_in_p, b_in_p, col_scale, batch, seq):
    t, d = x2d.shape
    tm = min(1024, seq)
    tiles_per_batch = seq // tm
    col = lambda rows: pl.BlockSpec((rows, TN), lambda i, j: (0, j))
    return pl.pallas_call(
        _in_proj_kernel,
        grid=(t // tm, Z_TILES),
        in_specs=[pl.BlockSpec((tm, d), lambda i, j: (i, 0)),
                  pl.BlockSpec((1, 6, d), lambda i, j: (i // tiles_per_batch, 0, 0)),
                  col(d), col(1), col(1)],
        out_specs=[pl.BlockSpec((tm, TN), lambda i, j: (i, j)),
                   pl.BlockSpec((tm, LANES), lambda i, j: (i, 0)),
                   pl.BlockSpec((tm, d), lambda i, j: (i, 0))],
        out_shape=[jax.ShapeDtypeStruct((t, Z_TILES * TN), BF16),
                   jax.ShapeDtypeStruct((t, LANES), F32),
                   jax.ShapeDtypeStruct((t, d), BF16)],
        compiler_params=_cparams(("arbitrary", "arbitrary")),
        name="in_projection",
    )(x2d, mod_l, w_in_p, b_in_p, col_scale)


def _dil_proj_kernel(h_ref, w_ref, b_ref, cs_ref, rc_ref, rp_ref, rm_ref, o_ref, de_scr, *, tm, dil):
    acc = (jnp.dot(h_ref[...], w_ref[...], preferred_element_type=F32) + b_ref[...]) * cs_ref[...]
    rc, rp, rm = rc_ref[0], rp_ref[0], rm_ref[0]
    slabs = [_rope_slab(acc[:, s * LANES:(s + 1) * LANES], rc, rp, rm, PARTIAL_ROT // 2)
             for s in range(TN // LANES)]
    if dil == 1:
        o_ref[0, 0] = jnp.concatenate(slabs, axis=1).astype(BF16)
    else:
        for s in range(TN // LANES):
            de_scr[s] = slabs[s]
        for r in range(dil):
            o_ref[0, r] = jnp.concatenate(
                [de_scr[s, pl.ds(r, tm // dil, stride=dil), :] for s in range(TN // LANES)],
                axis=1).astype(BF16)


def dilated_projection(h, w_in_p, b_in_p, col_scale, tabs_qkv, g, batch, seq):
    t, d = h.shape
    dil = DILATION_GROUPS[g][1]
    tm = min(1024, seq)
    tiles_per_batch = seq // tm
    col = lambda rows: pl.BlockSpec((rows, TN), lambda i, p: (0, TILE_B0 + p * N_DIL + g))
    tab = pl.BlockSpec((1, tm, LANES), lambda i, p: (p // 2, i, 0))
    return pl.pallas_call(
        functools.partial(_dil_proj_kernel, tm=tm, dil=dil),
        grid=(t // tm, 3),
        in_specs=[pl.BlockSpec((tm, d), lambda i, p: (i, 0)), col(d), col(1), col(1), tab, tab, tab],
        out_specs=pl.BlockSpec((1, dil, tm // dil, TN),
                               lambda i, p: (i // tiles_per_batch, 0, i % tiles_per_batch, p)),
        out_shape=jax.ShapeDtypeStruct((batch, dil, seq // dil, 3 * TN), BF16),
        scratch_shapes=[pltpu.VMEM((TN // LANES, tm, LANES), F32)],
        compiler_params=_cparams(("arbitrary", "arbitrary")),
        name="dilated_projection",
    )(h, w_in_p, b_in_p, col_scale, *tabs_qkv)


def pack_in_weights(w_in_l, b_in_l):
    d = w_in_l.shape[0]
    o_af = 3 * W_A
    o_b = o_af + A_HEADS
    o_c = o_b + 3 * N_DIL * W_B
    o_gl = o_c + Q_LORA + KV_LORA + D_ROPE

    def build(m):
        cut = lambda a, b: lax.slice_in_dim(m, a, b, axis=1)
        zeros = lambda n: jnp.zeros((m.shape[0], n), m.dtype)
        return jnp.concatenate([cut(o_gl, o_gl + 3 * d), cut(0, o_af), cut(o_c, o_gl),
                                zeros(LANES - D_ROPE), cut(o_af, o_b), zeros(LANES - A_HEADS),
                                cut(o_b, o_c)], axis=1)

    return build(w_in_l.astype(BF16)), build(b_in_l[None, :])


def _logf_kernel(zf_ref, lf_ref, carry):
    @pl.when(pl.program_id(1) == 0)
    def _():
        carry[...] = jnp.zeros_like(carry)

    x = zf_ref[...]
    ls = jnp.minimum(x, 0.0) - jnp.log1p(jnp.exp(-jnp.abs(x)))
    ts = x.shape[0]
    row = lax.broadcasted_iota(jnp.int32, (ts, ts), 0)
    col = lax.broadcasted_iota(jnp.int32, (ts, ts), 1)
    tri = (row >= col).astype(F32)
    cs = jnp.dot(tri, ls, precision=HIGHEST, preferred_element_type=F32) + carry[0:1, :]
    lf_ref[...] = cs * LOG2E
    carry[...] = jnp.broadcast_to(cs[ts - 1:ts, :], carry.shape)


def log_forget_cumsum(zf, batch, seq):
    ts = min(256, seq)
    nt = seq // ts
    return pl.pallas_call(
        _logf_kernel,
        grid=(batch, nt),
        in_specs=[pl.BlockSpec((ts, LANES), lambda b, i: (b * nt + i, 0))],
        out_specs=pl.BlockSpec((ts, LANES), lambda b, i: (b * nt + i, 0)),
        out_shape=jax.ShapeDtypeStruct(zf.shape, F32),
        scratch_shapes=[pltpu.VMEM((SUBLANES, LANES), F32)],
        compiler_params=_cparams(("arbitrary", "arbitrary")),
        name="log_forget_cumsum",
    )(zf)


FLASH_TQ = 1024
FLASH_SUB = 512
V_PAD = 16


def _split3(x):
    hi = x.astype(BF16)
    r1 = x - hi.astype(F32)
    mid = r1.astype(BF16)
    lo = (r1 - mid.astype(F32)).astype(BF16)
    return hi, mid, lo


def _flash_kernel(*refs, decay, dv, seq):
    if decay:
        q_ref, k_ref, v_ref, lfq_ref, lfk_ref, o_ref, kext, vext, m_scr, acc_scr = refs
    else:
        q_ref, k_ref, v_ref, o_ref, vext, m_scr, acc_scr = refs
    qi = pl.program_id(2)
    tq = q_ref.shape[0]
    sub = min(FLASH_SUB, tq)
    n_sub = tq // sub
    dn = (((1,), (1,)), ((), ()))

    @pl.when(qi == 0)
    def _():
        lane = lax.broadcasted_iota(jnp.int32, (seq, LANES), 1)
        for c in range(seq // sub):
            cols = slice(c * sub, (c + 1) * sub)
            vext[0:dv, cols] = v_ref[cols, :].astype(F32).T.astype(BF16)
        ones_row = lax.broadcasted_iota(jnp.int32, (V_PAD, seq), 0) == 0
        vext[dv:, :] = jnp.where(ones_row, 1.0, 0.0).astype(BF16)
        if decay:
            hi, mid, lo = _split3(lfk_ref[0, 0])
            kext[:, :HEAD_DIM] = k_ref[...]
            kext[:, HEAD_DIM:] = jnp.where(
                lane == 0, -hi, jnp.where(lane == 1, -mid, jnp.where(
                    lane == 2, -lo, jnp.where(lane < 6, 1.0, 0.0).astype(BF16))))

    if decay:
        lane = lax.broadcasted_iota(jnp.int32, (tq, LANES), 1)
        hi, mid, lo = _split3(lfq_ref[0, 0])
        q_tail = jnp.where(lane == 3, hi, jnp.where(lane == 4, mid, jnp.where(
            lane == 5, lo, jnp.where(lane < 3, 1.0, 0.0).astype(BF16))))
        q_all = jnp.concatenate([q_ref[...], q_tail], axis=1)
        keys = kext
    else:
        q_all = q_ref[...]
        keys = k_ref

    m_scr[...] = jnp.full_like(m_scr, -jnp.inf)
    acc_scr[...] = jnp.zeros_like(acc_scr)

    def chain(r, c, masked):
        cols = slice(r * sub, (r + 1) * sub)
        start = pl.multiple_of(c * sub, sub)
        s = lax.dot_general(keys[pl.ds(start, sub), :], q_all[cols], dn, preferred_element_type=F32)
        if masked:
            key = lax.broadcasted_iota(jnp.int32, s.shape, 0)
            qry = lax.broadcasted_iota(jnp.int32, s.shape, 1)
            s = jnp.where(key <= qry, s, -jnp.inf)
        m_prev = m_scr[0:1, cols]
        m_new = jnp.maximum(m_prev, jnp.max(s, axis=0, keepdims=True))
        p = jnp.exp2(s - m_new).astype(BF16)
        acc_scr[:, cols] = (jnp.exp2(m_prev - m_new) * acc_scr[:, cols]
                            + jnp.dot(vext[:, pl.ds(start, sub)], p, preferred_element_type=F32))
        m_scr[:, cols] = jnp.broadcast_to(m_new, (SUBLANES, sub))

    def full_chunks(c, carry):
        for r in range(n_sub):
            chain(r, c, False)
        return carry

    lax.fori_loop(0, qi * n_sub, full_chunks, 0)
    for d in range(n_sub):
        c = qi * n_sub + d
        for r in range(d, n_sub):
            chain(r, c, r == d)
    for r in range(n_sub):
        cols = slice(r * sub, (r + 1) * sub)
        o_t = acc_scr[0:dv, cols] / acc_scr[dv:dv + 1, cols]
        o_ref[cols, :] = o_t.T.astype(o_ref.dtype)


def flash_attention(q_arr, k_arr, v_arr, q_col0, k_col0, v_col0, dq, dv, heads, batch, seq,
                    decay=None):
    tq = min(FLASH_TQ, seq)
    nq = seq // tq
    in_specs = [
        pl.BlockSpec((tq, dq), lambda b, h, i: (b * nq + i, q_col0 + h)),
        pl.BlockSpec((seq, dq), lambda b, h, i: (b, k_col0 + h)),
        pl.BlockSpec((seq, dv), lambda b, h, i: (b, v_col0 + h)),
    ]
    args = [q_arr, k_arr, v_arr]
    scratch = [pltpu.VMEM((dv + V_PAD, seq), BF16), pltpu.VMEM((SUBLANES, tq), F32),
               pltpu.VMEM((dv + V_PAD, tq), F32)]
    if decay is not None:
        in_specs += [pl.BlockSpec((1, 1, tq, 1), lambda b, h, i: (b, h, i, 0)),
                     pl.BlockSpec((1, 1, seq, 1), lambda b, h, i: (b, h, 0, 0))]
        args += [decay, decay]
        scratch = [pltpu.VMEM((seq, dq + LANES), BF16)] + scratch
    return pl.pallas_call(
        functools.partial(_flash_kernel, decay=decay is not None, dv=dv, seq=seq),
        grid=(batch, heads, nq),
        in_specs=in_specs,
        out_specs=pl.BlockSpec((tq, dv), lambda b, h, i: (b * nq + i, h)),
        out_shape=jax.ShapeDtypeStruct((batch * seq, heads * dv), BF16),
        scratch_shapes=scratch,
        compiler_params=_cparams(("arbitrary", "arbitrary", "arbitrary")),
        name="flash_fox" if decay is not None else "flash_mla",
    )(*args)


def _dilated_kernel(q_ref, kc_ref, kp_ref, vc_ref, vp_ref, o_ref, lse_ref):
    n = pl.program_id(2)
    wb = WIN_BLOCK
    row = lax.broadcasted_iota(jnp.int32, (wb, wb), 0)
    col = lax.broadcasted_iota(jnp.int32, (wb, wb), 1)
    cur_ok = col <= row
    prev_ok = (col >= row) & (n > 0)
    dn = (((1,), (1,)), ((), ()))
    for h in range(B_HEADS):
        sl = slice(h * HEAD_DIM, (h + 1) * HEAD_DIM)
        q = q_ref[0, 0, :, sl]
        s_c = lax.dot_general(q, kc_ref[0, 0, :, sl], dn, preferred_element_type=F32)
        s_p = lax.dot_general(q, kp_ref[0, 0, :, sl], dn, preferred_element_type=F32)
        s_c = jnp.where(cur_ok, s_c, -jnp.inf)
        s_p = jnp.where(prev_ok, s_p, -jnp.inf)
        m = jnp.maximum(jnp.max(s_c, axis=1, keepdims=True), jnp.max(s_p, axis=1, keepdims=True))
        p_c = jnp.exp(s_c - m)
        p_p = jnp.exp(s_p - m)
        l = jnp.sum(p_c, axis=1, keepdims=True) + jnp.sum(p_p, axis=1, keepdims=True)
        o = (jnp.dot(p_c.astype(BF16), vc_ref[0, 0, :, sl], preferred_element_type=F32)
             + jnp.dot(p_p.astype(BF16), vp_ref[0, 0, :, sl], preferred_element_type=F32))
        o_ref[0, 0, :, sl] = (o / l).astype(BF16)
        lse_ref[0, 0, :, sl] = jnp.broadcast_to(m + jnp.log(l), (wb, HEAD_DIM))


def dilated_attention(zb, batch):
    _, dil, l_sub, _ = zb.shape
    nb = l_sub // WIN_BLOCK
    blk = (1, 1, WIN_BLOCK, W_B)
    cur = lambda c: pl.BlockSpec(blk, lambda b, r, n: (b, r, n, c))
    prev = lambda c: pl.BlockSpec(blk, lambda b, r, n: (b, r, jnp.maximum(n - 1, 0), c))
    out = pl.BlockSpec(blk, lambda b, r, n: (b, r, n, 0))
    return pl.pallas_call(
        _dilated_kernel,
        grid=(batch, dil, nb),
        in_specs=[cur(0), cur(1), prev(1), cur(2), prev(2)],
        out_specs=[out, out],
        out_shape=[jax.ShapeDtypeStruct((batch, dil, l_sub, W_B), BF16),
                   jax.ShapeDtypeStruct((batch, dil, l_sub, W_B), F32)],
        compiler_params=_cparams(("arbitrary", "arbitrary", "arbitrary")),
        name="dilated_attention",
    )(zb, zb, zb, zb, zb)


def _rms(x, g):
    return x * lax.rsqrt(jnp.mean(x * x, axis=-1, keepdims=True) + LN_EPS) * g


def _mla_up_kernel(cq_ref, ckv_ref, gq_ref, gkv_ref, wq_ref, wk_ref, wv_ref, rc_ref, rp_ref, rm_ref,
                   q_ref, k_ref, v_ref):
    scale = (D_NOPE + D_ROPE) ** -0.5 * LOG2E
    half = D_ROPE // 2
    rc, rp, rm = rc_ref[...], rp_ref[...], rm_ref[...]
    cqn = _rms(cq_ref[...].astype(F32), gq_ref[...]).astype(BF16)
    q = jnp.dot(cqn, wq_ref[...], preferred_element_type=F32) * scale
    ckvr = ckv_ref[...].astype(F32)
    ckvn = _rms(ckvr[:, :KV_LORA], gkv_ref[...]).astype(BF16)
    k_rope = _rope_slab(ckvr[:, KV_LORA:KV_LORA + LANES], rc, rp, rm, half).astype(BF16)
    k_nope = jnp.dot(ckvn, wk_ref[...], preferred_element_type=F32)
    v_ref[...] = jnp.dot(ckvn, wv_ref[...], preferred_element_type=F32).astype(BF16)
    for h in range(C_HEADS):
        lo = h * 2 * LANES
        q_ref[:, lo:lo + LANES] = q[:, lo:lo + LANES].astype(BF16)
        q_ref[:, lo + LANES:lo + 2 * LANES] = _rope_slab(
            q[:, lo + LANES:lo + 2 * LANES], rc, rp, rm, half).astype(BF16)
        k_ref[:, lo:lo + LANES] = k_nope[:, h * LANES:(h + 1) * LANES].astype(BF16)
        k_ref[:, lo + LANES:lo + 2 * LANES] = k_rope


def mla_up_projection(z, g_cq, g_ckv, wq_p, wk_p, wv_p, tabs_c):
    t = z.shape[0]
    tm = min(512, t)
    rc, rp, rm = tabs_c
    full = lambda a: pl.BlockSpec(a.shape, lambda i: (0,) * a.ndim)
    tab = pl.BlockSpec((tm, LANES), lambda i: (i, 0))
    row = lambda w: pl.BlockSpec((tm, w), lambda i: (i, 0))
    return pl.pallas_call(
        _mla_up_kernel,
        grid=(t // tm,),
        in_specs=[pl.BlockSpec((tm, TN), lambda i: (i, TILE_CQ)),
                  pl.BlockSpec((tm, TN), lambda i: (i, TILE_CKV)),
                  full(g_cq), full(g_ckv), full(wq_p), full(wk_p), full(wv_p), tab, tab, tab],
        out_specs=[row(C_HEADS * 2 * LANES), row(C_HEADS * 2 * LANES), row(W_C)],
        out_shape=[jax.ShapeDtypeStruct((t, C_HEADS * 2 * LANES), BF16),
                   jax.ShapeDtypeStruct((t, C_HEADS * 2 * LANES), BF16),
                   jax.ShapeDtypeStruct((t, W_C), BF16)],
        compiler_params=_cparams(("arbitrary",)),
        name="mla_up_projection",
    )(z, z, g_cq, g_ckv, wq_p, wk_p, wv_p, rc, rp, rm)


def pack_mla_weights(w_uq_l, w_ukv_l):
    wq = w_uq_l.reshape(Q_LORA, C_HEADS, D_NOPE + D_ROPE)
    wq = jnp.pad(wq, ((0, 0), (0, 0), (0, 2 * LANES - D_NOPE - D_ROPE)))
    wkv = w_ukv_l.reshape(KV_LORA, C_HEADS, D_NOPE + D_VC)
    wk = wkv[:, :, :D_NOPE].reshape(KV_LORA, C_HEADS * D_NOPE)
    wv = wkv[:, :, D_NOPE:].reshape(KV_LORA, W_C)
    return (wq.reshape(Q_LORA, C_HEADS * 2 * LANES).astype(BF16), wk.astype(BF16), wv.astype(BF16))


def _merge_kernel(gla_ref, glb_ref, glc_ref, oa_ref, ob0_ref, ob1_ref, ob2_ref,
                  ls0_ref, ls1_ref, ls2_ref, oc_ref, wa_ref, wb_ref, wc_ref, y_ref,
                  o_scr, l_scr, *, tm):
    for g, (o_ref, ls_ref) in enumerate(((ob0_ref, ls0_ref), (ob1_ref, ls1_ref), (ob2_ref, ls2_ref))):
        dil = DILATION_GROUPS[g][1]
        for h in range(B_HEADS):
            sl = slice(h * HEAD_DIM, (h + 1) * HEAD_DIM)
            for r in range(dil):
                rows = pl.ds(r, tm // dil, stride=dil) if dil > 1 else slice(None)
                o_scr[g * B_HEADS + h, rows, :] = o_ref[0, r, :, sl].astype(F32)
                l_scr[g * B_HEADS + h, rows, :] = ls_ref[0, r, :, sl]
    slabs = []
    for h in range(B_HEADS):
        l0, l1, l2 = l_scr[h], l_scr[B_HEADS + h], l_scr[2 * B_HEADS + h]
        m = jnp.maximum(jnp.maximum(l0, l1), l2)
        e0, e1, e2 = jnp.exp(l0 - m), jnp.exp(l1 - m), jnp.exp(l2 - m)
        den = e0 + e1 + e2
        slabs.append((e0 / den) * o_scr[h] + (e1 / den) * o_scr[B_HEADS + h]
                     + (e2 / den) * o_scr[2 * B_HEADS + h])
    o_b = jnp.concatenate(slabs, axis=1).astype(BF16)
    pa = jnp.dot(oa_ref[...], wa_ref[...], preferred_element_type=F32)
    pb = jnp.dot(o_b, wb_ref[...], preferred_element_type=F32)
    pc = jnp.dot(oc_ref[...], wc_ref[...], preferred_element_type=F32)
    y = (jax.nn.sigmoid(gla_ref[...].astype(F32)) * pa
         + jax.nn.sigmoid(glb_ref[...].astype(F32)) * pb
         + jax.nn.sigmoid(glc_ref[...].astype(F32)) * pc)
    y_ref[...] = y.astype(BF16)


def branch_merge(z, o_a, obs, lses, o_c, wa, wb, wc, batch, seq):
    t = z.shape[0]
    d = wa.shape[1]
    tm = min(512, seq)
    tpb = seq // tm
    full = lambda a: pl.BlockSpec(a.shape, lambda i: (0,) * a.ndim)
    gl = lambda c: pl.BlockSpec((tm, d), lambda i: (i, c))
    res = lambda dil: pl.BlockSpec((1, dil, tm // dil, W_B), lambda i: (i // tpb, 0, i % tpb, 0))
    res_specs = [res(dil) for _, dil in DILATION_GROUPS]
    return pl.pallas_call(
        functools.partial(_merge_kernel, tm=tm),
        grid=(t // tm,),
        in_specs=[gl(0), gl(1), gl(2), pl.BlockSpec((tm, W_A), lambda i: (i, 0))]
        + res_specs + res_specs
        + [pl.BlockSpec((tm, W_C), lambda i: (i, 0)), full(wa), full(wb), full(wc)],
        out_specs=pl.BlockSpec((tm, d), lambda i: (i, 0)),
        out_shape=jax.ShapeDtypeStruct((t, d), BF16),
        scratch_shapes=[pltpu.VMEM((N_DIL * B_HEADS, tm, HEAD_DIM), F32),
                        pltpu.VMEM((N_DIL * B_HEADS, tm, HEAD_DIM), F32)],
        compiler_params=_cparams(("arbitrary",)),
        name="branch_merge",
    )(z, z, z, o_a, *obs, *lses, o_c, wa, wb, wc)


def _layer_norm(x, g, b):
    mu = jnp.mean(x, axis=-1, keepdims=True)
    xc = x - mu
    var = jnp.mean(xc * xc, axis=-1, keepdims=True)
    return xc * lax.rsqrt(var + LN_EPS) * g + b


def _out_router_kernel(y_ref, x_ref, mod_ref, wo_ref, lng_ref, lnb_ref, wr_ref, br_ref,
                       x1_ref, h2_ref, ri_ref, rf_ref, cnt_ref, carry, *, tm, alpha):
    @pl.when(pl.program_id(0) == 0)
    def _():
        carry[...] = jnp.zeros_like(carry)

    gate1 = mod_ref[0, 2:3, :]
    shift2 = mod_ref[0, 3:4, :]
    scale2 = mod_ref[0, 4:5, :]
    mix = jnp.dot(y_ref[...], wo_ref[...], preferred_element_type=F32)
    x1 = _layer_norm(alpha * x_ref[...] + (1.0 + gate1) * mix, lng_ref[...], lnb_ref[...])
    x1_ref[...] = x1
    h2 = x1 * (1.0 + scale2) + shift2
    d = h2.shape[1]
    for c in range(d // LANES):
        h2_ref[pl.ds(c, tm, stride=d // LANES), :] = h2[:, c * LANES:(c + 1) * LANES]

    h_hi = h2.astype(BF16)
    h_lo = (h2 - h_hi.astype(F32)).astype(BF16)
    logits = (jnp.dot(h_hi, wr_ref[0], preferred_element_type=F32)
              + jnp.dot(h_lo, wr_ref[0], preferred_element_type=F32)
              + jnp.dot(h_hi, wr_ref[1], preferred_element_type=F32))
    s = jax.nn.sigmoid(logits.T[:N_EXPERTS])
    sel = s + br_ref[...]
    sel_m = [sel[i * N_GROUPS:(i + 1) * N_GROUPS] for i in range(EXPERTS_PER_GROUP)]
    s_m = [s[i * N_GROUPS:(i + 1) * N_GROUPS] for i in range(EXPERTS_PER_GROUP)]
    gscore = None
    for a in range(EXPERTS_PER_GROUP):
        for b in range(a + 1, EXPERTS_PER_GROUP):
            pair = sel_m[a] + sel_m[b]
            gscore = pair if gscore is None else jnp.maximum(gscore, pair)
    grp = lax.broadcasted_iota(jnp.int32, (N_GROUPS, tm), 0).astype(F32)
    gmax = jnp.max(gscore, axis=0, keepdims=True)
    gbest = jnp.min(jnp.where(gscore == gmax, grp, float(N_GROUPS)), axis=0, keepdims=True)
    in_best = grp == gbest
    v = [jnp.sum(jnp.where(in_best, m, 0.0), axis=0, keepdims=True) for m in sel_m]
    u = [jnp.sum(jnp.where(in_best, m, 0.0), axis=0, keepdims=True) for m in s_m]

    def first_argmax(vals):
        best = vals[0]
        for x in vals[1:]:
            best = jnp.maximum(best, x)
        idx = jnp.full_like(gbest, float(EXPERTS_PER_GROUP - 1))
        for i in range(EXPERTS_PER_GROUP - 2, -1, -1):
            idx = jnp.where(vals[i] == best, float(i), idx)
        return idx

    i1 = first_argmax(v)
    i2 = first_argmax([jnp.where(i1 == i, -jnp.inf, v[i]) for i in range(EXPERTS_PER_GROUP)])

    def member(vals, idx):
        out = vals[EXPERTS_PER_GROUP - 1]
        for i in range(EXPERTS_PER_GROUP - 2, -1, -1):
            out = jnp.where(idx == i, vals[i], out)
        return out

    u1, u2 = member(u, i1), member(u, i2)
    w1 = u1 / (u1 + u2)
    w2 = u2 / (u1 + u2)
    e1 = gbest * EXPERTS_PER_GROUP + i1
    e2 = gbest * EXPERTS_PER_GROUP + i2
    row1 = i1 * N_GROUPS + gbest
    row2 = i2 * N_GROUPS + gbest

    erow = lax.broadcasted_iota(jnp.int32, (N_EXPERTS, tm), 0).astype(F32)
    hit1, hit2 = erow == row1, erow == row2
    onehot = (hit1 | hit2).astype(F32)
    t_row = lax.broadcasted_iota(jnp.int32, (tm, tm), 0)
    t_col = lax.broadcasted_iota(jnp.int32, (tm, tm), 1)
    earlier = (t_row < t_col).astype(BF16)
    cnt = jnp.dot(onehot.astype(BF16), earlier, preferred_element_type=F32) + carry[:, 0:1]
    r1 = jnp.sum(jnp.where(hit1, cnt, 0.0), axis=0, keepdims=True)
    r2 = jnp.sum(jnp.where(hit2, cnt, 0.0), axis=0, keepdims=True)
    new_carry = carry[...] + jnp.sum(onehot, axis=1, keepdims=True)
    carry[...] = new_carry
    cnt_ref[...] = new_carry

    sub = lax.broadcasted_iota(jnp.int32, (SUBLANES, tm), 0)
    ri = jnp.where(sub == 0, e1, jnp.where(sub == 1, e2, jnp.where(sub == 2, r1, jnp.where(
        sub == 3, r2, 0.0))))
    ri_ref[0] = ri.astype(jnp.int32)
    rf_ref[0] = jnp.where(sub == 0, w1, jnp.where(sub == 1, w2, 0.0))


def out_proj_router(y, x2d, mod_l, w_o, ln_g, ln_b, wr_p, br_p, batch, seq, alpha):
    t, d = x2d.shape
    tm = min(256, seq)
    tpb = seq // tm
    nt = t // tm
    full = lambda a: pl.BlockSpec(a.shape, lambda i: (0,) * a.ndim)
    tile = pl.BlockSpec((tm, d), lambda i: (i, 0))
    rows = pl.BlockSpec((1, SUBLANES, tm), lambda i: (i, 0, 0))
    rt = d // LANES
    return pl.pallas_call(
        functools.partial(_out_router_kernel, tm=tm, alpha=alpha),
        grid=(nt,),
        in_specs=[tile, tile, pl.BlockSpec((1, 6, d), lambda i: (i // tpb, 0, 0)),
                  full(w_o), full(ln_g), full(ln_b), full(wr_p), full(br_p)],
        out_specs=[tile, pl.BlockSpec((tm * rt, LANES), lambda i: (i, 0)), rows, rows,
                   pl.BlockSpec((N_EXPERTS, LANES), lambda i: (0, 0))],
        out_shape=[jax.ShapeDtypeStruct((t, d), F32),
                   jax.ShapeDtypeStruct((t * rt, LANES), F32),
                   jax.ShapeDtypeStruct((nt, SUBLANES, tm), jnp.int32),
                   jax.ShapeDtypeStruct((nt, SUBLANES, tm), F32),
                   jax.ShapeDtypeStruct((N_EXPERTS, LANES), F32)],
        scratch_shapes=[pltpu.VMEM((N_EXPERTS, LANES), F32)],
        compiler_params=_cparams(("arbitrary",)),
        name="out_proj_router",
    )(y, x2d, mod_l, w_o, ln_g, ln_b, wr_p, br_p)


def _dispatch_kernel(dest_ref, h_ref, xs_in_ref, xs_ref, sem, *, tm, rt):
    del xs_in_ref

    def row_copy(a, t):
        return pltpu.make_async_copy(h_ref.at[pl.ds(t * rt, rt), :], xs_ref.at[dest_ref[0, 0, a]], sem)

    def issue(t, carry):
        row_copy(t, t).start()
        row_copy(tm + t, t).start()
        return carry

    def drain(t, carry):
        row_copy(t, t).wait()
        row_copy(tm + t, t).wait()
        return carry

    lax.fori_loop(0, tm, issue, 0, unroll=8)
    lax.fori_loop(0, tm, drain, 0, unroll=8)


def moe_dispatch(dest_tiles, h2t, xs_buf, tm):
    nt = dest_tiles.shape[0]
    rt = h2t.shape[0] // (nt * tm)
    return pl.pallas_call(
        functools.partial(_dispatch_kernel, tm=tm, rt=rt),
        grid=(nt,),
        in_specs=[pl.BlockSpec((1, 1, 2 * tm), lambda i: (i, 0, 0), memory_space=pltpu.SMEM),
                  pl.BlockSpec((tm * rt, LANES), lambda i: (i, 0)),
                  pl.BlockSpec(memory_space=pl.ANY)],
        out_specs=pl.BlockSpec(memory_space=pl.ANY),
        out_shape=jax.ShapeDtypeStruct(xs_buf.shape, xs_buf.dtype),
        scratch_shapes=[pltpu.SemaphoreType.DMA(())],
        input_output_aliases={2: 0},
        compiler_params=_cparams(("arbitrary",)),
        name="moe_dispatch",
    )(dest_tiles, h2t, xs_buf)


def _expert_kernel(be_ref, nu_ref, xs_ref, wg_ref, wu_ref, wd_ref, ys_ref, wg_s, wu_s, wd_s, *, rt):
    n = pl.program_id(0)

    @pl.when(n < nu_ref[0])
    def _():
        prev = be_ref[jnp.maximum(n - 1, 0)]

        @pl.when((n == 0) | (be_ref[n] != prev))
        def _():
            wg_s[...] = wg_ref[0, 0].astype(BF16)
            wu_s[...] = wu_ref[0, 0].astype(BF16)
            wd_s[...] = wd_ref[0, 0].astype(BF16)

        x = jnp.concatenate([xs_ref[pl.ds(c, MOE_BLOCK, stride=rt), :] for c in range(rt)],
                            axis=1).astype(BF16)
        gate = jnp.dot(x, wg_s[...], preferred_element_type=F32)
        up = jnp.dot(x, wu_s[...], preferred_element_type=F32)
        act = (gate * jax.nn.sigmoid(gate) * up).astype(BF16)
        y = jnp.dot(act, wd_s[...], preferred_element_type=F32)
        for c in range(rt):
            ys_ref[pl.ds(c, MOE_BLOCK, stride=rt), :] = y[:, c * LANES:(c + 1) * LANES]

    @pl.when(n >= nu_ref[0])
    def _():
        ys_ref[...] = jnp.zeros_like(ys_ref)


def expert_ffn(blk_e, n_used, xs2d, w_gate, w_up, w_down, layer):
    depth, n_exp, d, de = w_gate.shape
    rt = d // LANES
    n_blk = xs2d.shape[0] // (MOE_BLOCK * rt)
    rows = lambda n, be, nu: (jnp.minimum(n, nu[0] - 1), 0)
    grid_spec = pltpu.PrefetchScalarGridSpec(
        num_scalar_prefetch=2,
        grid=(n_blk,),
        in_specs=[pl.BlockSpec((MOE_BLOCK * rt, LANES), rows),
                  pl.BlockSpec((1, 1, d, de), lambda n, be, nu: (layer, be[n], 0, 0)),
                  pl.BlockSpec((1, 1, d, de), lambda n, be, nu: (layer, be[n], 0, 0)),
                  pl.BlockSpec((1, 1, de, d), lambda n, be, nu: (layer, be[n], 0, 0))],
        out_specs=pl.BlockSpec((MOE_BLOCK * rt, LANES), lambda n, be, nu: (n, 0)),
        scratch_shapes=[pltpu.VMEM((d, de), BF16), pltpu.VMEM((d, de), BF16),
                        pltpu.VMEM((de, d), BF16)])
    return pl.pallas_call(
        functools.partial(_expert_kernel, rt=rt),
        grid_spec=grid_spec,
        out_shape=jax.ShapeDtypeStruct(xs2d.shape, F32),
        compiler_params=_cparams(("arbitrary",)),
        name="expert_ffn",
    )(blk_e, n_used, xs2d, w_gate, w_up, w_down)


def _combine_kernel(dest_ref, dnext_ref, ys_ref, rf_ref, x_ref, mod_ref, lng_ref, lnb_ref, o_ref,
                    buf, sem, *, tm, rt, alpha, nt):
    i = pl.program_id(0)
    slot = i % 2

    def row_copy(dref, a, sl):
        return pltpu.make_async_copy(ys_ref.at[dref[0, 0, a]], buf.at[sl, pl.ds(a * rt, rt), :],
                                     sem.at[sl])

    def issue(dref, sl):
        def body(a, carry):
            row_copy(dref, a, sl).start()
            return carry
        lax.fori_loop(0, 2 * tm, body, 0, unroll=8)

    @pl.when(i == 0)
    def _():
        issue(dest_ref, 0)

    @pl.when(i + 1 < nt)
    def _():
        issue(dnext_ref, 1 - slot)

    def drain(a, carry):
        row_copy(dest_ref, a, slot).wait()
        return carry

    lax.fori_loop(0, 2 * tm, drain, 0, unroll=8)
    cur = buf.at[slot]

    def rows(k):
        return jnp.concatenate(
            [cur[pl.ds(k * tm * rt + c, tm, stride=rt), :] for c in range(rt)], axis=1)

    ff = rf_ref[:, 0:1] * rows(0) + rf_ref[:, 1:2] * rows(1)
    gate2 = mod_ref[0, 5:6, :]
    o_ref[...] = _layer_norm(alpha * x_ref[...] + (1.0 + gate2) * ff, lng_ref[...], lnb_ref[...])


def moe_combine(dest_tiles, ys3d, rf, x1, mod_l, ln_g, ln_b, batch, seq, tm, alpha):
    t, d = x1.shape
    rt = d // LANES
    tpb = seq // tm
    full = lambda a: pl.BlockSpec(a.shape, lambda i: (0,) * a.ndim)
    tile = pl.BlockSpec((tm, d), lambda i: (i, 0))
    nt = t // tm
    return pl.pallas_call(
        functools.partial(_combine_kernel, tm=tm, rt=rt, alpha=alpha, nt=nt),
        grid=(nt,),
        in_specs=[pl.BlockSpec((1, 1, 2 * tm), lambda i: (i, 0, 0), memory_space=pltpu.SMEM),
                  pl.BlockSpec((1, 1, 2 * tm), lambda i: (jnp.minimum(i + 1, nt - 1), 0, 0),
                               memory_space=pltpu.SMEM),
                  pl.BlockSpec(memory_space=pl.ANY),
                  pl.BlockSpec((tm, TOP_K), lambda i: (i, 0)),
                  tile, pl.BlockSpec((1, 6, d), lambda i: (i // tpb, 0, 0)), full(ln_g), full(ln_b)],
        out_specs=tile,
        out_shape=jax.ShapeDtypeStruct((t, d), F32),
        scratch_shapes=[pltpu.VMEM((2, 2 * tm * rt, LANES), F32), pltpu.SemaphoreType.DMA((2,))],
        compiler_params=_cparams(("arbitrary",)),
        name="moe_combine",
    )(dest_tiles, dest_tiles, ys3d, rf, x1, mod_l, ln_g, ln_b)


def _routing_tables(ri, cnt, n_rows, tm):
    nt = ri.shape[0]
    counts = cnt[:, 0].astype(jnp.int32)
    counts = counts.reshape(EXPERTS_PER_GROUP, N_GROUPS).T.reshape(N_EXPERTS)
    padded = (counts + MOE_BLOCK - 1) // MOE_BLOCK * MOE_BLOCK
    pend = jnp.cumsum(padded)
    pstart = pend - padded
    e_sel = ri[:, 0:TOP_K, :, None] == jnp.arange(N_EXPERTS, dtype=jnp.int32)
    dest = jnp.sum(jnp.where(e_sel, pstart, 0), axis=-1) + ri[:, TOP_K:2 * TOP_K, :]
    dest_tiles = dest.reshape(nt, 1, TOP_K * tm)
    n_blk = n_rows // MOE_BLOCK
    blk_start = jnp.arange(n_blk, dtype=jnp.int32) * MOE_BLOCK
    blk_e = jnp.minimum(jnp.sum(pend[None, :] <= blk_start[:, None], axis=1), N_EXPERTS - 1)
    n_used = (pend[-1:] // MOE_BLOCK).astype(jnp.int32)
    return dest_tiles.astype(jnp.int32), blk_e.astype(jnp.int32), n_used


def kernel(x, c, positions, w_ada, b_ada, w_in, b_in, g_cq, g_ckv, w_uq, w_ukv, w_pa, w_pb, w_pc, w_o,
           ln1_g, ln1_b, w_router, b_router, w_gate, w_up, w_down, ln2_g, ln2_b):
    batch, seq, d = x.shape
    depth = w_ada.shape[0]
    t = batch * seq
    alpha = (2 * depth) ** 0.25
    rt = d // LANES
    assert rt == ROW_TILE and seq % 2048 == 0 and d % TN == 0
    tm_moe = min(256, seq)

    c_pad = jnp.zeros((SUBLANES, d), F32).at[:batch].set(c)
    mod_all = ada_modulation(c_pad, w_ada, b_ada)[:, :batch].reshape(depth, batch, 6, d)
    pos_col = positions.reshape(t, 1)
    tabs_b = rope_tables(pos_col, PARTIAL_ROT)
    tabs_c = rope_tables(pos_col, D_ROPE)
    tabs_qkv = [jnp.stack([tab, jnp.full_like(tab, fill)]) for tab, fill in zip(tabs_b, (1.0, 0.0, 0.0))]
    qk_scale = HEAD_DIM ** -0.5
    col_scale = jnp.ones((1, N_TILES * TN), F32)
    col_scale = col_scale.at[:, TILE_AQ * TN:(TILE_AQ + 1) * TN].set(qk_scale * LOG2E)
    col_scale = col_scale.at[:, TILE_B0 * TN:(TILE_B0 + N_DIL) * TN].set(qk_scale)

    perm = np.arange(N_EXPERTS).reshape(N_GROUPS, EXPERTS_PER_GROUP).T.reshape(-1)
    wr_f = jnp.zeros((d, LANES), F32).at[:, :N_EXPERTS].set(w_router.astype(F32)[:, perm])
    wr_hi = wr_f.astype(BF16)
    wr_p = jnp.stack([wr_hi, (wr_f - wr_hi.astype(F32)).astype(BF16)])
    br_p = b_router.astype(F32)[perm][:, None]

    n_rows = -(-(t * TOP_K + N_EXPERTS * (MOE_BLOCK - 1)) // MOE_BLOCK) * MOE_BLOCK
    xs_buf = jnp.zeros((n_rows, rt, LANES), F32)

    x2d = x.reshape(t, d)
    for l in range(depth):
        w_in_p, b_in_p = pack_in_weights(w_in[l], b_in[l])
        z, zf, h = in_projection(x2d, mod_all[l], w_in_p, b_in_p, col_scale, batch, seq)
        zbs = [dilated_projection(h, w_in_p, b_in_p, col_scale, tabs_qkv, g, batch, seq)
               for g in range(N_DIL)]

        lf = log_forget_cumsum(zf, batch, seq)[:, :A_HEADS].reshape(batch, seq, A_HEADS)
        lf = lf.transpose(0, 2, 1)[:, :, :, None]
        o_a = flash_attention(z, z, z, TILE_AQ * 4, TILE_AQ * 4 + 4, TILE_AQ * 4 + 8, HEAD_DIM,
                              HEAD_DIM, A_HEADS, batch, seq, decay=lf)

        obs, lses = [], []
        for zb in zbs:
            o_g, lse_g = dilated_attention(zb, batch)
            obs.append(o_g)
            lses.append(lse_g)

        wq_p, wk_p, wv_p = pack_mla_weights(w_uq[l], w_ukv[l])
        q_c, k_c, v_c = mla_up_projection(z, g_cq[l][None, :], g_ckv[l][None, :], wq_p, wk_p, wv_p,
                                          tabs_c)
        o_c = flash_attention(q_c, k_c, v_c, 0, 0, 0, 2 * LANES, D_VC, C_HEADS, batch, seq)

        y = branch_merge(z, o_a, obs, lses, o_c, w_pa[l].astype(BF16), w_pb[l].astype(BF16),
                         w_pc[l].astype(BF16), batch, seq)
        x1, h2t, ri, rf, cnt = out_proj_router(y, x2d, mod_all[l], w_o[l].astype(BF16),
                                               ln1_g[l][None, :], ln1_b[l][None, :], wr_p, br_p,
                                               batch, seq, alpha)

        dest_tiles, blk_e, n_used = _routing_tables(ri, cnt, n_rows, tm_moe)
        wts = rf[:, 0:TOP_K, :].transpose(0, 2, 1).reshape(t, TOP_K)
        xs_buf = moe_dispatch(dest_tiles, h2t, xs_buf, tm_moe)
        ys = expert_ffn(blk_e, n_used, xs_buf.reshape(n_rows * rt, LANES), w_gate, w_up, w_down, l)
        x2d = moe_combine(dest_tiles, ys.reshape(n_rows, rt, LANES), wts, x1, mod_all[l],
                          ln2_g[l][None, :], ln2_b[l][None, :], batch, seq, tm_moe, alpha)
    return x2d.reshape(batch, seq, d)
```

```python
import functools

import jax
import jax.numpy as jnp
import numpy as np
from jax import lax
from jax.experimental import pallas as pl
from jax.experimental.pallas import tpu as pltpu

F32 = jnp.float32
BF16 = jnp.bfloat16
HIGHEST = lax.Precision.HIGHEST

LANES = 128
SUBLANES = 8
VMEM_LIMIT = 56 * 1024 * 1024

HEAD_DIM = 128
A_HEADS = 4
B_HEADS = 4
DILATION_GROUPS = ((128, 1), (512, 4), (2048, 16))
N_DIL = 3
C_HEADS = 6
Q_LORA = 512
KV_LORA = 256
D_NOPE = 128
D_ROPE = 64
D_VC = 128
ROPE_THETA = 500000.0
PARTIAL_ROT = HEAD_DIM // 4
WIN_BLOCK = 128
N_EXPERTS = 32
N_GROUPS = 8
EXPERTS_PER_GROUP = N_EXPERTS // N_GROUPS
TOP_K = 2
LN_EPS = 1e-5
LOG2E = 1.4426950408889634
W_A =A_HEADS * HEAD_DIM
W_B = B_HEADS * HEAD_DIM
W_C = C_HEADS * D_VC

TN = 512
TILE_AQ = 12
TILE_CQ = 15
TILE_CKV = 16
TILE_B0 = 17
N_TILES = 26
Z_TILES = 17
AF_LANE0 = KV_LORA + LANES
ROW_TILE = 16
MOE_BLOCK = 512


def _cparams(sem):
    return pltpu.CompilerParams(dimension_semantics=sem, vmem_limit_bytes=VMEM_LIMIT)


def _ada_kernel(c_ref, w_ref, b_ref, o_ref):
    o_ref[0] = jnp.dot(c_ref[...], w_ref[0], precision=HIGHEST,
                       preferred_element_type=F32) + b_ref[0]


def ada_modulation(c_pad, w_ada, b_ada):
    depth, d, n = w_ada.shape
    tn = 1024
    return pl.pallas_call(
        _ada_kernel,
        grid=(depth, n // tn),
        in_specs=[pl.BlockSpec((SUBLANES, d), lambda l, j: (0, 0)),
                  pl.BlockSpec((1, d, tn), lambda l, j: (l, 0, j)),
                  pl.BlockSpec((1, 1, tn), lambda l, j: (l, 0, j))],
        out_specs=pl.BlockSpec((1, SUBLANES, tn), lambda l, j: (l, 0, j)),
        out_shape=jax.ShapeDtypeStruct((depth, SUBLANES, n), F32),
        compiler_params=_cparams(("arbitrary", "arbitrary")),
        name="ada_modulation",
    )(c_pad, w_ada, b_ada.reshape(depth, 1, n))


def _rope_table_kernel(pos_ref, invf_ref, mhi_ref, mlo_ref, c_ref, sp_ref, sm_ref):
    ang = pos_ref[...].astype(F32) * invf_ref[...]
    sin = jnp.sin(ang)
    c_ref[...] = jnp.cos(ang)
    sp_ref[...] = sin * mhi_ref[...]
    sm_ref[...] = -sin * mlo_ref[...]


def rope_tables(pos_col, rot_dim):
    t = pos_col.shape[0]
    half = rot_dim // 2
    inv_freq = 1.0 / (ROPE_THETA ** (jnp.arange(0, rot_dim, 2, dtype=F32) / rot_dim))
    lane = np.arange(LANES)
    invf = jnp.where(lane < rot_dim, jnp.tile(inv_freq, LANES // half), 0.0).astype(F32)[None, :]
    mhi = jnp.asarray(((lane >= half) & (lane < rot_dim)).astype(np.float32))[None, :]
    mlo = jnp.asarray((lane < half).astype(np.float32))[None, :]
    tt = min(t, 1024)
    row = pl.BlockSpec((1, LANES), lambda i: (0, 0))
    out = pl.BlockSpec((tt, LANES), lambda i: (i, 0))
    return pl.pallas_call(
        _rope_table_kernel,
        grid=(t // tt,),
        in_specs=[pl.BlockSpec((tt, 1), lambda i: (i, 0)), row, row, row],
        out_specs=[out, out, out],
        out_shape=[jax.ShapeDtypeStruct((t, LANES), F32)] * 3,
        compiler_params=_cparams(("arbitrary",)),
        name="rope_tables",
    )(pos_col, invf, mhi, mlo)


def _rope_slab(x, c, sp, sm, half):
    return x * c + pltpu.roll(x, half, 1) * sp + pltpu.roll(x, LANES - half, 1) * sm


def _in_proj_kernel(x_ref, mod_ref, w_ref, b_ref, cs_ref, z_ref, zf_ref, h_ref):
    j = pl.program_id(1)

    @pl.when(j == 0)
    def _():
        shift = mod_ref[0, 0:1, :]
        scale = mod_ref[0, 1:2, :]
        h_ref[...] = (x_ref[...] * (1.0 + scale) + shift).astype(BF16)

    acc = (jnp.dot(h_ref[...], w_ref[0], preferred_element_type=F32) + b_ref[0]) * cs_ref[...]
    z_ref[...] = acc.astype(BF16)

    @pl.when(j == TILE_CKV)
    def _():
        zf_ref[...] = acc[:, AF_LANE0:AF_LANE0 + LANES]


def in_projection(x2d, mod_l, w_in_p, b_in_p, col_scale, layer, batch, seq):
    t, d = x2d.shape
    tm = min(1024, seq)
    tiles_per_batch = seq // tm
    col = lambda rows: pl.BlockSpec((1, rows, TN), lambda i, j: (layer, 0, j))
    return pl.pallas_call(
        _in_proj_kernel,
        grid=(t // tm, Z_TILES),
        in_specs=[pl.BlockSpec((tm, d), lambda i, j: (i, 0)),
                  pl.BlockSpec((1, 6, d), lambda i, j: (i // tiles_per_batch, 0, 0)),
                  col(d), col(1), pl.BlockSpec((1, TN), lambda i, j: (0, j))],
        out_specs=[pl.BlockSpec((tm, TN), lambda i, j: (i, j)),
                   pl.BlockSpec((tm, LANES), lambda i, j: (i, 0)),
                   pl.BlockSpec((tm, d), lambda i, j: (i, 0))],
        out_shape=[jax.ShapeDtypeStruct((t, Z_TILES * TN), BF16),
                   jax.ShapeDtypeStruct((t, LANES), F32),
                   jax.ShapeDtypeStruct((t, d), BF16)],
        compiler_params=_cparams(("arbitrary", "arbitrary")),
        name="in_projection",
    )(x2d, mod_l, w_in_p, b_in_p, col_scale)


def _dil_proj_kernel(h_ref, w_ref, b_ref, cs_ref, rc_ref, rp_ref, rm_ref, o_ref, de_scr, *, tm, dil):
    half = tm // 2
    for part in range(2):
        rows = slice(part * half, (part + 1) * half)
        acc = (jnp.dot(h_ref[rows, :], w_ref[0], preferred_element_type=F32) + b_ref[0]) * cs_ref[...]
        rc, rp, rm = rc_ref[0, rows, :], rp_ref[0, rows, :], rm_ref[0, rows, :]
        slabs = [_rope_slab(acc[:, s * LANES:(s + 1) * LANES], rc, rp, rm, PARTIAL_ROT // 2)
                 for s in range(TN // LANES)]
        if dil == 1:
            o_ref[0, 0, rows, :] = jnp.concatenate(slabs, axis=1).astype(BF16)
        else:
            n = half // dil
            for s in range(TN // LANES):
                de_scr[part * (TN // LANES) + s] = slabs[s]
            for r in range(dil):
                o_ref[0, r, part * n:(part + 1) * n, :] = jnp.concatenate(
                    [de_scr[part * (TN // LANES) + s, pl.ds(r, n, stride=dil), :]
                     for s in range(TN // LANES)], axis=1).astype(BF16)


def dilated_projection(h, w_in_p, b_in_p, col_scale, tabs_qkv, g, layer, batch, seq):
    t, d = h.shape
    dil = DILATION_GROUPS[g][1]
    tm = min(1024, seq)
    tiles_per_batch = seq // tm
    col = lambda rows: pl.BlockSpec((1, rows, TN), lambda i, p: (layer, 0, TILE_B0 + p * N_DIL + g))
    tab = pl.BlockSpec((1, tm, LANES), lambda i, p: (p // 2, i, 0))
    return pl.pallas_call(
        functools.partial(_dil_proj_kernel, tm=tm, dil=dil),
        grid=(t // tm, 3),
        in_specs=[pl.BlockSpec((tm, d), lambda i, p: (i, 0)), col(d), col(1),
                  pl.BlockSpec((1, TN), lambda i, p: (0, TILE_B0 + p * N_DIL + g)), tab, tab, tab],
        out_specs=pl.BlockSpec((1, dil, tm // dil, TN),
                               lambda i, p: (i // tiles_per_batch, 0, i % tiles_per_batch, p)),
        out_shape=jax.ShapeDtypeStruct((batch, dil, seq // dil, 3 * TN), BF16),
        scratch_shapes=[pltpu.VMEM((2 * TN // LANES, tm // 2, LANES), F32)],
        compiler_params=_cparams(("arbitrary", "arbitrary")),
        name="dilated_projection",
    )(h, w_in_p, b_in_p, col_scale, *tabs_qkv)


def _in_segments(d):
    o_af = 3 * W_A
    o_b = o_af + A_HEADS
    o_c = o_b + 3 * N_DIL * W_B
    o_gl = o_c + Q_LORA + KV_LORA + D_ROPE
    c_width = o_gl - o_c
    return ((o_gl, 3 * d, 0), (0, o_af, TILE_AQ * TN), (o_c, c_width, TILE_CQ * TN),
            (o_af, A_HEADS, TILE_CKV * TN + AF_LANE0), (o_b, o_c - o_b, TILE_B0 * TN))


def _pack_kernel(w_ref, o_ref, *, segments):
    o_ref[...] = jnp.zeros_like(o_ref)
    for src, width, dst in segments:
        o_ref[0, :, dst:dst + width] = w_ref[0, :, src:src + width].astype(o_ref.dtype)


def pack_in_weights(w_in, out_dtype, rows):
    depth, d_rows, n_in = w_in.shape
    segments = _in_segments((n_in - (3 * W_A + A_HEADS + 3 * N_DIL * W_B + Q_LORA + KV_LORA + D_ROPE))
                            // 3)
    return pl.pallas_call(
        functools.partial(_pack_kernel, segments=segments),
        grid=(depth, d_rows // rows),
        in_specs=[pl.BlockSpec((1, rows, n_in), lambda l, i: (l, i, 0))],
        out_specs=pl.BlockSpec((1, rows, N_TILES * TN), lambda l, i: (l, i, 0)),
        out_shape=jax.ShapeDtypeStruct((depth, d_rows, N_TILES * TN), out_dtype),
        compiler_params=_cparams(("arbitrary", "arbitrary")),
        name="pack_in_weights",
    )(w_in)


def _logf_kernel(zf_ref, lf_ref, carry):
    @pl.when(pl.program_id(1) == 0)
    def _():
        carry[...] = jnp.zeros_like(carry)

    x = zf_ref[...]
    ls = jnp.minimum(x, 0.0) - jnp.log1p(jnp.exp(-jnp.abs(x)))
    ts = x.shape[0]
    row = lax.broadcasted_iota(jnp.int32, (ts, ts), 0)
    col = lax.broadcasted_iota(jnp.int32, (ts, ts), 1)
    tri = (row >= col).astype(F32)
    cs = jnp.dot(tri, ls, precision=HIGHEST, preferred_element_type=F32) + carry[0:1, :]
    lf_ref[...] = cs * LOG2E
    carry[...] = jnp.broadcast_to(cs[ts - 1:ts, :], carry.shape)


def log_forget_cumsum(zf, batch, seq):
    ts = min(256, seq)
    nt = seq // ts
    return pl.pallas_call(
        _logf_kernel,
        grid=(batch, nt),
        in_specs=[pl.BlockSpec((ts, LANES), lambda b, i: (b * nt + i, 0))],
        out_specs=pl.BlockSpec((ts, LANES), lambda b, i: (b * nt + i, 0)),
        out_shape=jax.ShapeDtypeStruct(zf.shape, F32),
        scratch_shapes=[pltpu.VMEM((SUBLANES, LANES), F32)],
        compiler_params=_cparams(("arbitrary", "arbitrary")),
        name="log_forget_cumsum",
    )(zf)


FLASH_TQ = 2048
FLASH_SUB = 512
V_PAD = 16


def _split3(x):
    hi = x.astype(BF16)
    r1 = x - hi.astype(F32)
    mid = r1.astype(BF16)
    lo = (r1 - mid.astype(F32)).astype(BF16)
    return hi, mid, lo


def _flash_kernel(*refs, decay, dv, seq):
    if decay:
        q_ref, k_ref, v_ref, lfq_ref, lfk_ref, o_ref, kext, vext, m_scr, acc_scr = refs
    else:
        q_ref, k_ref, v_ref, o_ref, vext, m_scr, acc_scr = refs
    qi = pl.program_id(2)
    tq = q_ref.shape[0]
    sub = min(FLASH_SUB, tq)
    n_sub = tq // sub
    dn = (((1,), (1,)), ((), ()))

    @pl.when(qi == 0)
    def _():
        lane = lax.broadcasted_iota(jnp.int32, (seq, LANES), 1)
        for c in range(seq // sub):
            cols = slice(c * sub, (c + 1) * sub)
            vext[0:dv, cols] = v_ref[cols, :].astype(F32).T.astype(BF16)
        ones_row = lax.broadcasted_iota(jnp.int32, (V_PAD, seq), 0) == 0
        vext[dv:, :] = jnp.where(ones_row, 1.0, 0.0).astype(BF16)
        if decay:
            hi, mid, lo = _split3(lfk_ref[0, 0])
            kext[:, :HEAD_DIM] = k_ref[...]
            kext[:, HEAD_DIM:] = jnp.where(
                lane == 0, -hi, jnp.where(lane == 1, -mid, jnp.where(
                    lane == 2, -lo, jnp.where(lane < 6, 1.0, 0.0).astype(BF16))))

    if decay:
        lane = lax.broadcasted_iota(jnp.int32, (tq, LANES), 1)
        hi, mid, lo = _split3(lfq_ref[0, 0])
        q_tail = jnp.where(lane == 3, hi, jnp.where(lane == 4, mid, jnp.where(
            lane == 5, lo, jnp.where(lane < 3, 1.0, 0.0).astype(BF16))))
        q_all = jnp.concatenate([q_ref[...], q_tail], axis=1)
        keys = kext
    else:
        q_all = q_ref[...]
        keys = k_ref

    m_scr[...] = jnp.full_like(m_scr, -jnp.inf)
    acc_scr[...] = jnp.zeros_like(acc_scr)

    def update(work):
        cols = [slice(r * sub, (r + 1) * sub) for r, _, _ in work]
        starts = [pl.multiple_of(c * sub, sub) for _, c, _ in work]
        scores = []
        for (r, c, masked), cl, st in zip(work, cols, starts):
            s = lax.dot_general(keys[pl.ds(st, sub), :], q_all[cl], dn, preferred_element_type=F32)
            if masked:
                key = lax.broadcasted_iota(jnp.int32, s.shape, 0)
                qry = lax.broadcasted_iota(jnp.int32, s.shape, 1)
                s = jnp.where(key <= qry, s, -jnp.inf)
            scores.append(s)
        m_prev = [m_scr[0:1, cl] for cl in cols]
        m_new = [jnp.maximum(mp, jnp.max(s, axis=0, keepdims=True)) for mp, s in zip(m_prev, scores)]
        probs = [jnp.exp2(s - mn).astype(BF16) for s, mn in zip(scores, m_new)]
        for cl, st, mp, mn, p in zip(cols, starts, m_prev, m_new, probs):
            acc_scr[:, cl] = (jnp.exp2(mp - mn) * acc_scr[:, cl]
                              + jnp.dot(vext[:, pl.ds(st, sub)], p, preferred_element_type=F32))
            m_scr[:, cl] = jnp.broadcast_to(mn, (SUBLANES, sub))

    def full_chunks(c, carry):
        update([(r, c, False) for r in range(n_sub)])
        return carry

    lax.fori_loop(0, qi * n_sub, full_chunks, 0)
    for d in range(n_sub):
        update([(r, qi * n_sub + d, r == d) for r in range(d, n_sub)])
    for r in range(n_sub):
        cols = slice(r * sub, (r + 1) * sub)
        o_t = acc_scr[0:dv, cols] / acc_scr[dv:dv + 1, cols]
        o_ref[cols, :] = o_t.T.astype(o_ref.dtype)


def flash_attention(q_arr, k_arr, v_arr, q_col0, k_col0, v_col0, dq, dv, heads, batch, seq,
                    decay=None):
    tq = min(FLASH_TQ, seq)
    nq = seq // tq
    in_specs = [
        pl.BlockSpec((tq, dq), lambda b, h, i: (b * nq + i, q_col0 + h)),
        pl.BlockSpec((seq, dq), lambda b, h, i: (b, k_col0 + h)),
        pl.BlockSpec((seq, dv), lambda b, h, i: (b, v_col0 + h)),
    ]
    args = [q_arr, k_arr, v_arr]
    scratch = [pltpu.VMEM((dv + V_PAD, seq), BF16), pltpu.VMEM((SUBLANES, tq), F32),
               pltpu.VMEM((dv + V_PAD, tq), F32)]
    if decay is not None:
        in_specs += [pl.BlockSpec((1, 1, tq, 1), lambda b, h, i: (b, h, i, 0)),
                     pl.BlockSpec((1, 1, seq, 1), lambda b, h, i: (b, h, 0, 0))]
        args += [decay, decay]
        scratch = [pltpu.VMEM((seq, dq + LANES), BF16)] + scratch
    return pl.pallas_call(
        functools.partial(_flash_kernel, decay=decay is not None, dv=dv, seq=seq),
        grid=(batch, heads, nq),
        in_specs=in_specs,
        out_specs=pl.BlockSpec((tq, dv), lambda b, h, i: (b * nq + i, h)),
        out_shape=jax.ShapeDtypeStruct((batch * seq, heads * dv), BF16),
        scratch_shapes=scratch,
        compiler_params=_cparams(("arbitrary", "arbitrary", "arbitrary")),
        name="flash_fox" if decay is not None else "flash_mla",
    )(*args)


DIL_ROWS = 512


def _dilated_kernel(q_ref, k_ref, v_ref, o_ref, lse_ref, *, rows):
    c = pl.program_id(2)
    wb = WIN_BLOCK
    row = lax.broadcasted_iota(jnp.int32, (wb, wb), 0)
    col = lax.broadcasted_iota(jnp.int32, (wb, wb), 1)
    cur_ok = col <= row
    dn = (((1,), (1,)), ((), ()))
    chains = []
    for n in range(rows // wb):
        base = c * rows + n * wb
        cur = pl.ds(pl.multiple_of(base, wb), wb)
        prev = pl.ds(pl.multiple_of(jnp.maximum(base - wb, 0), wb), wb)
        prev_ok = (col >= row) & (base > 0)
        for h in range(B_HEADS):
            chains.append((slice(n * wb, (n + 1) * wb), slice(h * HEAD_DIM, (h + 1) * HEAD_DIM),
                           cur, prev, prev_ok))
    scores = []
    for qrows, sl, cur, prev, prev_ok in chains:
        q = q_ref[0, 0, qrows, sl]
        s_c = lax.dot_general(q, k_ref[0, 0, cur, sl], dn, preferred_element_type=F32)
        s_p = lax.dot_general(q, k_ref[0, 0, prev, sl], dn, preferred_element_type=F32)
        scores.append((jnp.where(cur_ok, s_c, -jnp.inf), jnp.where(prev_ok, s_p, -jnp.inf)))
    maxes = [jnp.maximum(jnp.max(s_c, axis=1, keepdims=True), jnp.max(s_p, axis=1, keepdims=True))
             for s_c, s_p in scores]
    probs = [(jnp.exp(s_c - m), jnp.exp(s_p - m)) for (s_c, s_p), m in zip(scores, maxes)]
    sums = [jnp.sum(p_c, axis=1, keepdims=True) + jnp.sum(p_p, axis=1, keepdims=True)
            for p_c, p_p in probs]
    for (qrows, sl, cur, prev, _), (p_c, p_p), m, l in zip(chains, probs, maxes, sums):
        o = (jnp.dot(p_c.astype(BF16), v_ref[0, 0, cur, sl], preferred_element_type=F32)
             + jnp.dot(p_p.astype(BF16), v_ref[0, 0, prev, sl], preferred_element_type=F32))
        o_ref[0, 0, qrows, sl] = (o / l).astype(BF16)
        lse_ref[0, 0, qrows, sl] = jnp.broadcast_to(m + jnp.log(l), (wb, HEAD_DIM))


def dilated_attention(zb, batch):
    _, dil, l_sub, _ = zb.shape
    rows = min(DIL_ROWS, l_sub)
    chunk = lambda c: pl.BlockSpec((1, 1, rows, W_B), lambda b, r, i: (b, r, i, c))
    whole = lambda c: pl.BlockSpec((1, 1, l_sub, W_B), lambda b, r, i: (b, r, 0, c))
    return pl.pallas_call(
        functools.partial(_dilated_kernel, rows=rows),
        grid=(batch, dil, l_sub // rows),
        in_specs=[chunk(0), whole(1), whole(2)],
        out_specs=[chunk(0), chunk(0)],
        out_shape=[jax.ShapeDtypeStruct((batch, dil, l_sub, W_B), BF16),
                   jax.ShapeDtypeStruct((batch, dil, l_sub, W_B), F32)],
        compiler_params=_cparams(("arbitrary", "arbitrary", "arbitrary")),
        name="dilated_attention",
    )(zb, zb, zb)


def _rms(x, g):
    return x * lax.rsqrt(jnp.mean(x * x, axis=-1, keepdims=True) + LN_EPS) * g


def _mla_up_kernel(cq_ref, ckv_ref, gq_ref, gkv_ref, wq_ref, wk_ref, wv_ref, rc_ref, rp_ref, rm_ref,
                   q_ref, k_ref, v_ref):
    scale = (D_NOPE + D_ROPE) ** -0.5 * LOG2E
    half = D_ROPE // 2
    rc, rp, rm = rc_ref[...], rp_ref[...], rm_ref[...]
    cqn = _rms(cq_ref[...].astype(F32), gq_ref[...]).astype(BF16)
    q = jnp.dot(cqn, wq_ref[...], preferred_element_type=F32) * scale
    ckvr = ckv_ref[...].astype(F32)
    ckvn = _rms(ckvr[:, :KV_LORA], gkv_ref[...]).astype(BF16)
    k_rope = _rope_slab(ckvr[:, KV_LORA:KV_LORA + LANES], rc, rp, rm, half).astype(BF16)
    k_nope = jnp.dot(ckvn, wk_ref[...], preferred_element_type=F32)
    v_ref[...] = jnp.dot(ckvn, wv_ref[...], preferred_element_type=F32).astype(BF16)
    for h in range(C_HEADS):
        lo = h * 2 * LANES
        q_ref[:, lo:lo + LANES] = q[:, lo:lo + LANES].astype(BF16)
        q_ref[:, lo + LANES:lo + 2 * LANES] = _rope_slab(
            q[:, lo + LANES:lo + 2 * LANES], rc, rp, rm, half).astype(BF16)
        k_ref[:, lo:lo + LANES] = k_nope[:, h * LANES:(h + 1) * LANES].astype(BF16)
        k_ref[:, lo + LANES:lo + 2 * LANES] = k_rope


def mla_up_projection(z, g_cq, g_ckv, wq_p, wk_p, wv_p, tabs_c):
    t = z.shape[0]
    tm = min(512, t)
    rc, rp, rm = tabs_c
    full = lambda a: pl.BlockSpec(a.shape, lambda i: (0,) * a.ndim)
    tab = pl.BlockSpec((tm, LANES), lambda i: (i, 0))
    row = lambda w: pl.BlockSpec((tm, w), lambda i: (i, 0))
    return pl.pallas_call(
        _mla_up_kernel,
        grid=(t // tm,),
        in_specs=[pl.BlockSpec((tm, TN), lambda i: (i, TILE_CQ)),
                  pl.BlockSpec((tm, TN), lambda i: (i, TILE_CKV)),
                  full(g_cq), full(g_ckv), full(wq_p), full(wk_p), full(wv_p), tab, tab, tab],
        out_specs=[row(C_HEADS * 2 * LANES), row(C_HEADS * 2 * LANES), row(W_C)],
        out_shape=[jax.ShapeDtypeStruct((t, C_HEADS * 2 * LANES), BF16),
                   jax.ShapeDtypeStruct((t, C_HEADS * 2 * LANES), BF16),
                   jax.ShapeDtypeStruct((t, W_C), BF16)],
        compiler_params=_cparams(("arbitrary",)),
        name="mla_up_projection",
    )(z, z, g_cq, g_ckv, wq_p, wk_p, wv_p, rc, rp, rm)


def pack_mla_weights(w_uq_l, w_ukv_l):
    wq = w_uq_l.reshape(Q_LORA, C_HEADS, D_NOPE + D_ROPE)
    wq = jnp.pad(wq, ((0, 0), (0, 0), (0, 2 * LANES - D_NOPE - D_ROPE)))
    wkv = w_ukv_l.reshape(KV_LORA, C_HEADS, D_NOPE + D_VC)
    wk = wkv[:, :, :D_NOPE].reshape(KV_LORA, C_HEADS * D_NOPE)
    wv = wkv[:, :, D_NOPE:].reshape(KV_LORA, W_C)
    return (wq.reshape(Q_LORA, C_HEADS * 2 * LANES).astype(BF16), wk.astype(BF16), wv.astype(BF16))


def _merge_kernel(gla_ref, glb_ref, glc_ref, oa_ref, ob0_ref, ob1_ref, ob2_ref,
                  ls0_ref, ls1_ref, ls2_ref, oc_ref, wa_ref, wb_ref, wc_ref, y_ref,
                  o_scr, l_scr, *, tm):
    for g, (o_ref, ls_ref) in enumerate(((ob0_ref, ls0_ref), (ob1_ref, ls1_ref), (ob2_ref, ls2_ref))):
        dil = DILATION_GROUPS[g][1]
        for h in range(B_HEADS):
            sl = slice(h * HEAD_DIM, (h + 1) * HEAD_DIM)
            for r in range(dil):
                rows = pl.ds(r, tm // dil, stride=dil) if dil > 1 else slice(None)
                o_scr[g * B_HEADS + h, rows, :] = o_ref[0, r, :, sl].astype(F32)
                l_scr[g * B_HEADS + h, rows, :] = ls_ref[0, r, :, sl]
    slabs = []
    for h in range(B_HEADS):
        l0, l1, l2 = l_scr[h], l_scr[B_HEADS + h], l_scr[2 * B_HEADS + h]
        m = jnp.maximum(jnp.maximum(l0, l1), l2)
        e0, e1, e2 = jnp.exp(l0 - m), jnp.exp(l1 - m), jnp.exp(l2 - m)
        den = e0 + e1 + e2
        slabs.append((e0 / den) * o_scr[h] + (e1 / den) * o_scr[B_HEADS + h]
                     + (e2 / den) * o_scr[2 * B_HEADS + h])
    o_b = jnp.concatenate(slabs, axis=1).astype(BF16)
    pa = jnp.dot(oa_ref[...], wa_ref[...], preferred_element_type=F32)
    pb = jnp.dot(o_b, wb_ref[...], preferred_element_type=F32)
    pc = jnp.dot(oc_ref[...], wc_ref[...], preferred_element_type=F32)
    y = (jax.nn.sigmoid(gla_ref[...].astype(F32)) * pa
         + jax.nn.sigmoid(glb_ref[...].astype(F32)) * pb
         + jax.nn.sigmoid(glc_ref[...].astype(F32)) * pc)
    y_ref[...] = y.astype(BF16)


def branch_merge(z, o_a, obs, lses, o_c, wa, wb, wc, batch, seq):
    t = z.shape[0]
    d = wa.shape[1]
    tm = min(512, seq)
    tpb = seq // tm
    full = lambda a: pl.BlockSpec(a.shape, lambda i: (0,) * a.ndim)
    gl = lambda c: pl.BlockSpec((tm, d), lambda i: (i, c))
    res = lambda dil: pl.BlockSpec((1, dil, tm // dil, W_B), lambda i: (i // tpb, 0, i % tpb, 0))
    res_specs = [res(dil) for _, dil in DILATION_GROUPS]
    return pl.pallas_call(
        functools.partial(_merge_kernel, tm=tm),
        grid=(t // tm,),
        in_specs=[gl(0), gl(1), gl(2), pl.BlockSpec((tm, W_A), lambda i: (i, 0))]
        + res_specs + res_specs
        + [pl.BlockSpec((tm, W_C), lambda i: (i, 0)), full(wa), full(wb), full(wc)],
        out_specs=pl.BlockSpec((tm, d), lambda i: (i, 0)),
        out_shape=jax.ShapeDtypeStruct((t, d), BF16),
        scratch_shapes=[pltpu.VMEM((N_DIL * B_HEADS, tm, HEAD_DIM), F32),
                        pltpu.VMEM((N_DIL * B_HEADS, tm, HEAD_DIM), F32)],
        compiler_params=_cparams(("arbitrary",)),
        name="branch_merge",
    )(z, z, z, o_a, *obs, *lses, o_c, wa, wb, wc)


def _layer_norm(x, g, b):
    mu = jnp.mean(x, axis=-1, keepdims=True)
    xc = x - mu
    var = jnp.mean(xc * xc, axis=-1, keepdims=True)
    return xc * lax.rsqrt(var + LN_EPS) * g + b


def _out_router_kernel(y_ref, x_ref, mod_ref, wo_ref, lng_ref, lnb_ref, wr_ref, br_ref,
                       x1_ref, h2_ref, ri_ref, rf_ref, cnt_ref, carry, *, tm, alpha):
    @pl.when(pl.program_id(0) == 0)
    def _():
        carry[...] = jnp.zeros_like(carry)

    gate1 = mod_ref[0, 2:3, :]
    shift2 = mod_ref[0, 3:4, :]
    scale2 = mod_ref[0, 4:5, :]
    mix = jnp.dot(y_ref[...], wo_ref[...], preferred_element_type=F32)
    x1 = _layer_norm(alpha * x_ref[...] + (1.0 + gate1) * mix, lng_ref[...], lnb_ref[...])
    x1_ref[...] = x1
    h2 = x1 * (1.0 + scale2) + shift2
    d = h2.shape[1]
    for c in range(d // LANES):
        h2_ref[pl.ds(c, tm, stride=d // LANES), :] = h2[:, c * LANES:(c + 1) * LANES]

    h_hi = h2.astype(BF16)
    h_lo = (h2 - h_hi.astype(F32)).astype(BF16)
    logits = (jnp.dot(h_hi, wr_ref[0], preferred_element_type=F32)
              + jnp.dot(h_lo, wr_ref[0], preferred_element_type=F32)
              + jnp.dot(h_hi, wr_ref[1], preferred_element_type=F32))
    s = jax.nn.sigmoid(logits.T[:N_EXPERTS])
    sel = s + br_ref[...]
    sel_m = [sel[i * N_GROUPS:(i + 1) * N_GROUPS] for i in range(EXPERTS_PER_GROUP)]
    s_m = [s[i * N_GROUPS:(i + 1) * N_GROUPS] for i in range(EXPERTS_PER_GROUP)]
    gscore = None
    for a in range(EXPERTS_PER_GROUP):
        for b in range(a + 1, EXPERTS_PER_GROUP):
            pair = sel_m[a] + sel_m[b]
            gscore = pair if gscore is None else jnp.maximum(gscore, pair)
    grp = lax.broadcasted_iota(jnp.int32, (N_GROUPS, tm), 0).astype(F32)
    gmax = jnp.max(gscore, axis=0, keepdims=True)
    gbest = jnp.min(jnp.where(gscore == gmax, grp, float(N_GROUPS)), axis=0, keepdims=True)
    in_best = grp == gbest
    v = [jnp.sum(jnp.where(in_best, m, 0.0), axis=0, keepdims=True) for m in sel_m]
    u = [jnp.sum(jnp.where(in_best, m, 0.0), axis=0, keepdims=True) for m in s_m]

    def first_argmax(vals):
        best = vals[0]
        for x in vals[1:]:
            best = jnp.maximum(best, x)
        idx = jnp.full_like(gbest, float(EXPERTS_PER_GROUP - 1))
        for i in range(EXPERTS_PER_GROUP - 2, -1, -1):
            idx = jnp.where(vals[i] == best, float(i), idx)
        return idx

    i1 = first_argmax(v)
    i2 = first_argmax([jnp.where(i1 == i, -jnp.inf, v[i]) for i in range(EXPERTS_PER_GROUP)])

    def member(vals, idx):
        out = vals[EXPERTS_PER_GROUP - 1]
        for i in range(EXPERTS_PER_GROUP - 2, -1, -1):
            out = jnp.where(idx == i, vals[i], out)
        return out

    u1, u2 = member(u, i1), member(u, i2)
    w1 = u1 / (u1 + u2)
    w2 = u2 / (u1 + u2)
    e1 = gbest * EXPERTS_PER_GROUP + i1
    e2 = gbest * EXPERTS_PER_GROUP + i2
    row1 = i1 * N_GROUPS + gbest
    row2 = i2 * N_GROUPS + gbest

    erow = lax.broadcasted_iota(jnp.int32, (N_EXPERTS, tm), 0).astype(F32)
    hit1, hit2 = erow == row1, erow == row2
    onehot = (hit1 | hit2).astype(F32)
    t_row = lax.broadcasted_iota(jnp.int32, (tm, tm), 0)
    t_col = lax.broadcasted_iota(jnp.int32, (tm, tm), 1)
    earlier = (t_row < t_col).astype(BF16)
    cnt = jnp.dot(onehot.astype(BF16), earlier, preferred_element_type=F32) + carry[:, 0:1]
    r1 = jnp.sum(jnp.where(hit1, cnt, 0.0), axis=0, keepdims=True)
    r2 = jnp.sum(jnp.where(hit2, cnt, 0.0), axis=0, keepdims=True)
    new_carry = carry[...] + jnp.sum(onehot, axis=1, keepdims=True)
    carry[...] = new_carry
    cnt_ref[...] = new_carry

    sub = lax.broadcasted_iota(jnp.int32, (SUBLANES, tm), 0)
    ri = jnp.where(sub == 0, e1, jnp.where(sub == 1, e2, jnp.where(sub == 2, r1, jnp.where(
        sub == 3, r2, 0.0))))
    ri_ref[0] = ri.astype(jnp.int32)
    rf_ref[0] = jnp.where(sub == 0, w1, jnp.where(sub == 1, w2, 0.0))


def out_proj_router(y, x2d, mod_l, w_o, ln_g, ln_b, wr_p, br_p, batch, seq, alpha):
    t, d = x2d.shape
    tm = min(256, seq)
    tpb = seq // tm
    nt = t // tm
    full = lambda a: pl.BlockSpec(a.shape, lambda i: (0,) * a.ndim)
    tile = pl.BlockSpec((tm, d), lambda i: (i, 0))
    rows = pl.BlockSpec((1, SUBLANES, tm), lambda i: (i, 0, 0))
    rt = d // LANES
    return pl.pallas_call(
        functools.partial(_out_router_kernel, tm=tm, alpha=alpha),
        grid=(nt,),
        in_specs=[tile, tile, pl.BlockSpec((1, 6, d), lambda i: (i // tpb, 0, 0)),
                  full(w_o), full(ln_g), full(ln_b), full(wr_p), full(br_p)],
        out_specs=[tile, pl.BlockSpec((tm * rt, LANES), lambda i: (i, 0)), rows, rows,
                   pl.BlockSpec((N_EXPERTS, LANES), lambda i: (0, 0))],
        out_shape=[jax.ShapeDtypeStruct((t, d), F32),
                   jax.ShapeDtypeStruct((t * rt, LANES), F32),
                   jax.ShapeDtypeStruct((nt, SUBLANES, tm), jnp.int32),
                   jax.ShapeDtypeStruct((nt, SUBLANES, tm), F32),
                   jax.ShapeDtypeStruct((N_EXPERTS, LANES), F32)],
        scratch_shapes=[pltpu.VMEM((N_EXPERTS, LANES), F32)],
        compiler_params=_cparams(("arbitrary",)),
        name="out_proj_router",
    )(y, x2d, mod_l, w_o, ln_g, ln_b, wr_p, br_p)


def _dispatch_kernel(dest_ref, h_ref, xs_in_ref, xs_ref, sem, *, tm, rt):
    del xs_in_ref

    def row_copy(a, t):
        return pltpu.make_async_copy(h_ref.at[pl.ds(t * rt, rt), :], xs_ref.at[dest_ref[0, 0, a]], sem)

    def issue(t, carry):
        row_copy(t, t).start()
        row_copy(tm + t, t).start()
        return carry

    def drain(t, carry):
        row_copy(t, t).wait()
        row_copy(tm + t, t).wait()
        return carry

    lax.fori_loop(0, tm, issue, 0, unroll=8)
    lax.fori_loop(0, tm, drain, 0, unroll=8)


def moe_dispatch(dest_tiles, h2t, xs_buf, tm):
    nt = dest_tiles.shape[0]
    rt = h2t.shape[0] // (nt * tm)
    return pl.pallas_call(
        functools.partial(_dispatch_kernel, tm=tm, rt=rt),
        grid=(nt,),
        in_specs=[pl.BlockSpec((1, 1, 2 * tm), lambda i: (i, 0, 0), memory_space=pltpu.SMEM),
                  pl.BlockSpec((tm * rt, LANES), lambda i: (i, 0)),
                  pl.BlockSpec(memory_space=pl.ANY)],
        out_specs=pl.BlockSpec(memory_space=pl.ANY),
        out_shape=jax.ShapeDtypeStruct(xs_buf.shape, xs_buf.dtype),
        scratch_shapes=[pltpu.SemaphoreType.DMA(())],
        input_output_aliases={2: 0},
        compiler_params=_cparams(("arbitrary",)),
        name="moe_dispatch",
    )(dest_tiles, h2t, xs_buf)


def _expert_kernel(be_ref, nu_ref, xs_ref, wg_ref, wu_ref, wd_ref, ys_ref, wg_s, wu_s, wd_s, *, rt):
    n = pl.program_id(0)

    @pl.when(n < nu_ref[0])
    def _():
        prev = be_ref[jnp.maximum(n - 1, 0)]

        @pl.when((n == 0) | (be_ref[n] != prev))
        def _():
            wg_s[...] = wg_ref[0, 0].astype(BF16)
            wu_s[...] = wu_ref[0, 0].astype(BF16)
            wd_s[...] = wd_ref[0, 0].astype(BF16)

        x = jnp.concatenate([xs_ref[pl.ds(c, MOE_BLOCK, stride=rt), :] for c in range(rt)],
                            axis=1).astype(BF16)
        gate = jnp.dot(x, wg_s[...], preferred_element_type=F32)
        up = jnp.dot(x, wu_s[...], preferred_element_type=F32)
        act = (gate * jax.nn.sigmoid(gate) * up).astype(BF16)
        y = jnp.dot(act, wd_s[...], preferred_element_type=F32)
        for c in range(rt):
            ys_ref[pl.ds(c, MOE_BLOCK, stride=rt), :] = y[:, c * LANES:(c + 1) * LANES]

    @pl.when(n >= nu_ref[0])
    def _():
        ys_ref[...] = jnp.zeros_like(ys_ref)


def expert_ffn(blk_e, n_used, xs2d, w_gate, w_up, w_down, layer):
    depth, n_exp, d, de = w_gate.shape
    rt = d // LANES
    n_blk = xs2d.shape[0] // (MOE_BLOCK * rt)
    rows = lambda n, be, nu: (jnp.minimum(n, nu[0] - 1), 0)
    grid_spec = pltpu.PrefetchScalarGridSpec(
        num_scalar_prefetch=2,
        grid=(n_blk,),
        in_specs=[pl.BlockSpec((MOE_BLOCK * rt, LANES), rows),
                  pl.BlockSpec((1, 1, d, de), lambda n, be, nu: (layer, be[n], 0, 0)),
                  pl.BlockSpec((1, 1, d, de), lambda n, be, nu: (layer, be[n], 0, 0)),
                  pl.BlockSpec((1, 1, de, d), lambda n, be, nu: (layer, be[n], 0, 0))],
        out_specs=pl.BlockSpec((MOE_BLOCK * rt, LANES), lambda n, be, nu: (n, 0)),
        scratch_shapes=[pltpu.VMEM((d, de), BF16), pltpu.VMEM((d, de), BF16),
                        pltpu.VMEM((de, d), BF16)])
    return pl.pallas_call(
        functools.partial(_expert_kernel, rt=rt),
        grid_spec=grid_spec,
        out_shape=jax.ShapeDtypeStruct(xs2d.shape, F32),
        compiler_params=_cparams(("arbitrary",)),
        name="expert_ffn",
    )(blk_e, n_used, xs2d, w_gate, w_up, w_down)


def _combine_kernel(dest_ref, dnext_ref, ys_ref, rf_ref, x_ref, mod_ref, lng_ref, lnb_ref, o_ref,
                    buf, sem, *, tm, rt, alpha, nt):
    i = pl.program_id(0)
    slot = i % 2

    def row_copy(dref, a, sl):
        return pltpu.make_async_copy(ys_ref.at[dref[0, 0, a]], buf.at[sl, pl.ds(a * rt, rt), :],
                                     sem.at[sl])

    def issue(dref, sl):
        def body(a, carry):
            row_copy(dref, a, sl).start()
            return carry
        lax.fori_loop(0, 2 * tm, body, 0, unroll=8)

    @pl.when(i == 0)
    def _():
        issue(dest_ref, 0)

    @pl.when(i + 1 < nt)
    def _():
        issue(dnext_ref, 1 - slot)

    def drain(a, carry):
        row_copy(dest_ref, a, slot).wait()
        return carry

    lax.fori_loop(0, 2 * tm, drain, 0, unroll=8)
    cur = buf.at[slot]

    def rows(k):
        return jnp.concatenate(
            [cur[pl.ds(k * tm * rt + c, tm, stride=rt), :] for c in range(rt)], axis=1)

    ff = rf_ref[:, 0:1] * rows(0) + rf_ref[:, 1:2] * rows(1)
    gate2 = mod_ref[0, 5:6, :]
    o_ref[...] = _layer_norm(alpha * x_ref[...] + (1.0 + gate2) * ff, lng_ref[...], lnb_ref[...])


def moe_combine(dest_tiles, ys3d, rf, x1, mod_l, ln_g, ln_b, batch, seq, tm, alpha):
    t, d = x1.shape
    rt = d // LANES
    tpb = seq // tm
    full = lambda a: pl.BlockSpec(a.shape, lambda i: (0,) * a.ndim)
    tile = pl.BlockSpec((tm, d), lambda i: (i, 0))
    nt = t // tm
    return pl.pallas_call(
        functools.partial(_combine_kernel, tm=tm, rt=rt, alpha=alpha, nt=nt),
        grid=(nt,),
        in_specs=[pl.BlockSpec((1, 1, 2 * tm), lambda i: (i, 0, 0), memory_space=pltpu.SMEM),
                  pl.BlockSpec((1, 1, 2 * tm), lambda i: (jnp.minimum(i + 1, nt - 1), 0, 0),
                               memory_space=pltpu.SMEM),
                  pl.BlockSpec(memory_space=pl.ANY),
                  pl.BlockSpec((tm, TOP_K), lambda i: (i, 0)),
                  tile, pl.BlockSpec((1, 6, d), lambda i: (i // tpb, 0, 0)), full(ln_g), full(ln_b)],
        out_specs=tile,
        out_shape=jax.ShapeDtypeStruct((t, d), F32),
        scratch_shapes=[pltpu.VMEM((2, 2 * tm * rt, LANES), F32), pltpu.SemaphoreType.DMA((2,))],
        compiler_params=_cparams(("arbitrary",)),
        name="moe_combine",
    )(dest_tiles, dest_tiles, ys3d, rf, x1, mod_l, ln_g, ln_b)


def _routing_tables(ri, cnt, n_rows, tm):
    nt = ri.shape[0]
    counts = cnt[:, 0].astype(jnp.int32)
    counts = counts.reshape(EXPERTS_PER_GROUP, N_GROUPS).T.reshape(N_EXPERTS)
    padded = (counts + MOE_BLOCK - 1) // MOE_BLOCK * MOE_BLOCK
    pend = jnp.cumsum(padded)
    pstart = pend - padded
    e_sel = ri[:, 0:TOP_K, :, None] == jnp.arange(N_EXPERTS, dtype=jnp.int32)
    dest = jnp.sum(jnp.where(e_sel, pstart, 0), axis=-1) + ri[:, TOP_K:2 * TOP_K, :]
    dest_tiles = dest.reshape(nt, 1, TOP_K * tm)
    n_blk = n_rows // MOE_BLOCK
    blk_start = jnp.arange(n_blk, dtype=jnp.int32) * MOE_BLOCK
    blk_e = jnp.minimum(jnp.sum(pend[None, :] <= blk_start[:, None], axis=1), N_EXPERTS - 1)
    n_used = (pend[-1:] // MOE_BLOCK).astype(jnp.int32)
    return dest_tiles.astype(jnp.int32), blk_e.astype(jnp.int32), n_used


def kernel(x, c, positions, w_ada, b_ada, w_in, b_in, g_cq, g_ckv, w_uq, w_ukv, w_pa, w_pb, w_pc, w_o,
           ln1_g, ln1_b, w_router, b_router, w_gate, w_up, w_down, ln2_g, ln2_b):
    batch, seq, d = x.shape
    depth = w_ada.shape[0]
    t = batch * seq
    alpha = (2 * depth) ** 0.25
    rt = d // LANES
    assert rt == ROW_TILE and seq % 2048 == 0 and d % TN == 0
    tm_moe = min(256, seq)

    c_pad = jnp.zeros((SUBLANES, d), F32).at[:batch].set(c)
    mod_all = ada_modulation(c_pad, w_ada, b_ada)[:, :batch].reshape(depth, batch, 6, d)
    pos_col = positions.reshape(t, 1)
    tabs_b = rope_tables(pos_col, PARTIAL_ROT)
    tabs_c = rope_tables(pos_col, D_ROPE)
    tabs_qkv = [jnp.stack([tab, jnp.full_like(tab, fill)]) for tab, fill in zip(tabs_b, (1.0, 0.0, 0.0))]
    qk_scale = HEAD_DIM ** -0.5
    col_scale = jnp.ones((1, N_TILES * TN), F32)
    col_scale = col_scale.at[:, TILE_AQ * TN:(TILE_AQ + 1) * TN].set(qk_scale * LOG2E)
    col_scale = col_scale.at[:, TILE_B0 * TN:(TILE_B0 + N_DIL) * TN].set(qk_scale)

    perm = np.arange(N_EXPERTS).reshape(N_GROUPS, EXPERTS_PER_GROUP).T.reshape(-1)
    wr_f = jnp.zeros((d, LANES), F32).at[:, :N_EXPERTS].set(w_router.astype(F32)[:, perm])
    wr_hi = wr_f.astype(BF16)
    wr_p = jnp.stack([wr_hi, (wr_f - wr_hi.astype(F32)).astype(BF16)])
    br_p = b_router.astype(F32)[perm][:, None]

    n_rows = -(-(t * TOP_K + N_EXPERTS * (MOE_BLOCK - 1)) // MOE_BLOCK) * MOE_BLOCK
    xs_buf = jnp.zeros((n_rows, rt, LANES), F32)

    w_in_p = pack_in_weights(w_in, BF16, rows=128)
    b_in_p = pack_in_weights(b_in[:, None, :], F32, rows=1)

    x2d = x.reshape(t, d)
    for l in range(depth):
        z, zf, h = in_projection(x2d, mod_all[l], w_in_p, b_in_p, col_scale, l, batch, seq)
        zbs = [dilated_projection(h, w_in_p, b_in_p, col_scale, tabs_qkv, g, l, batch, seq)
               for g in range(N_DIL)]

        lf = log_forget_cumsum(zf, batch, seq)[:, :A_HEADS].reshape(batch, seq, A_HEADS)
        lf = lf.transpose(0, 2, 1)[:, :, :, None]
        o_a = flash_attention(z, z, z, TILE_AQ * 4, TILE_AQ * 4 + 4, TILE_AQ * 4 + 8, HEAD_DIM,
                              HEAD_DIM, A_HEADS, batch, seq, decay=lf)

        obs, lses = [], []
        for zb in zbs:
            o_g, lse_g = dilated_attention(zb, batch)
            obs.append(o_g)
            lses.append(lse_g)

        wq_p, wk_p, wv_p = pack_mla_weights(w_uq[l], w_ukv[l])
        q_c, k_c, v_c = mla_up_projection(z, g_cq[l][None, :], g_ckv[l][None, :], wq_p, wk_p, wv_p,
                                          tabs_c)
        o_c = flash_attention(q_c, k_c, v_c, 0, 0, 0, 2 * LANES, D_VC, C_HEADS, batch, seq)

        y = branch_merge(z, o_a, obs, lses, o_c, w_pa[l].astype(BF16), w_pb[l].astype(BF16),
                         w_pc[l].astype(BF16), batch, seq)
        x1, h2t, ri, rf, cnt = out_proj_router(y, x2d, mod_all[l], w_o[l].astype(BF16),
                                               ln1_g[l][None, :], ln1_b[l][None, :], wr_p, br_p,
                                               batch, seq, alpha)

        dest_tiles, blk_e, n_used = _routing_tables(ri, cnt, n_rows, tm_moe)
        wts = rf[:, 0:TOP_K, :].transpose(0, 2, 1).reshape(t, TOP_K)
        xs_buf = moe_dispatch(dest_tiles, h2t, xs_buf, tm_moe)
        ys = expert_ffn(blk_e, n_used, xs_buf.reshape(n_rows * rt, LANES), w_gate, w_up, w_down, l)
        x2d = moe_combine(dest_tiles, ys.reshape(n_rows, rt, LANES), wts, x1, mod_all[l],
                          ln2_g[l][None, :], ln2_b[l][None, :], batch, seq, tm_moe, alpha)
    return x2d.reshape(batch, seq, d)
```

```python
import functools

import jax
import jax.numpy as jnp
import numpy as np
from jax import lax
from jax.experimental import pallas as pl
from jax.experimental.pallas import tpu as pltpu

F32 = jnp.float32
BF16 = jnp.bfloat16
HIGHEST = lax.Precision.HIGHEST

LANES = 128
SUBLANES = 8
VMEM_LIMIT = 56 * 1024 * 1024

HEAD_DIM = 128
A_HEADS = 4
B_HEADS = 4
DILATION_GROUPS = ((128, 1), (512, 4), (2048, 16))
N_DIL = 3
C_HEADS = 6
Q_LORA = 512
KV_LORA = 256
D_NOPE = 128
D_ROPE = 64
D_VC = 128
ROPE_THETA = 500000.0
PARTIAL_ROT = HEAD_DIM // 4
WIN_BLOCK = 128
N_EXPERTS = 32
N_GROUPS = 8
EXPERTS_PER_GROUP = N_EXPERTS // N_GROUPS
TOP_K = 2
LN_EPS = 1e-5
LOG2E = 1.4426950408889634
W_A =A_HEADS * HEAD_DIM
W_B = B_HEADS * HEAD_DIM
W_C = C_HEADS * D_VC

TN = 512
TILE_AQ = 12
TILE_CQ = 15
TILE_CKV = 16
TILE_B0 = 17
N_TILES = 26
Z_TILES = 17
AF_LANE0 = KV_LORA + LANES
ROW_TILE = 16
MOE_BLOCK = 512
MOE_PARTS = 2


def _cparams(sem):
    return pltpu.CompilerParams(dimension_semantics=sem, vmem_limit_bytes=VMEM_LIMIT)


def _ada_kernel(c_ref, w_ref, b_ref, o_ref):
    o_ref[0] = jnp.dot(c_ref[...], w_ref[0], precision=HIGHEST,
                       preferred_element_type=F32) + b_ref[0]


def ada_modulation(c_pad, w_ada, b_ada):
    depth, d, n = w_ada.shape
    tn = 1024
    return pl.pallas_call(
        _ada_kernel,
        grid=(depth, n // tn),
        in_specs=[pl.BlockSpec((SUBLANES, d), lambda l, j: (0, 0)),
                  pl.BlockSpec((1, d, tn), lambda l, j: (l, 0, j)),
                  pl.BlockSpec((1, 1, tn), lambda l, j: (l, 0, j))],
        out_specs=pl.BlockSpec((1, SUBLANES, tn), lambda l, j: (l, 0, j)),
        out_shape=jax.ShapeDtypeStruct((depth, SUBLANES, n), F32),
        compiler_params=_cparams(("arbitrary", "arbitrary")),
        name="ada_modulation",
    )(c_pad, w_ada, b_ada.reshape(depth, 1, n))


def _rope_table_kernel(pos_ref, invf_ref, mhi_ref, mlo_ref, c_ref, sp_ref, sm_ref):
    ang = pos_ref[...].astype(F32) * invf_ref[...]
    sin = jnp.sin(ang)
    c_ref[...] = jnp.cos(ang)
    sp_ref[...] = sin * mhi_ref[...]
    sm_ref[...] = -sin * mlo_ref[...]


def rope_tables(pos_col, rot_dim):
    t = pos_col.shape[0]
    half = rot_dim // 2
    inv_freq = 1.0 / (ROPE_THETA ** (jnp.arange(0, rot_dim, 2, dtype=F32) / rot_dim))
    lane = np.arange(LANES)
    invf = jnp.where(lane < rot_dim, jnp.tile(inv_freq, LANES // half), 0.0).astype(F32)[None, :]
    mhi = jnp.asarray(((lane >= half) & (lane < rot_dim)).astype(np.float32))[None, :]
    mlo = jnp.asarray((lane < half).astype(np.float32))[None, :]
    tt = min(t, 1024)
    row = pl.BlockSpec((1, LANES), lambda i: (0, 0))
    out = pl.BlockSpec((tt, LANES), lambda i: (i, 0))
    return pl.pallas_call(
        _rope_table_kernel,
        grid=(t // tt,),
        in_specs=[pl.BlockSpec((tt, 1), lambda i: (i, 0)), row, row, row],
        out_specs=[out, out, out],
        out_shape=[jax.ShapeDtypeStruct((t, LANES), F32)] * 3,
        compiler_params=_cparams(("arbitrary",)),
        name="rope_tables",
    )(pos_col, invf, mhi, mlo)


def _rope_slab(x, c, sp, sm, half):
    return x * c + pltpu.roll(x, half, 1) * sp + pltpu.roll(x, LANES - half, 1) * sm


def _in_proj_kernel(x_ref, mod_ref, w_ref, b_ref, cs_ref, z_ref, zf_ref, h_ref):
    j = pl.program_id(1)

    @pl.when(j == 0)
    def _():
        shift = mod_ref[0, 0:1, :]
        scale = mod_ref[0, 1:2, :]
        h_ref[...] = (x_ref[...] * (1.0 + scale) + shift).astype(BF16)

    acc = (jnp.dot(h_ref[...], w_ref[0], preferred_element_type=F32) + b_ref[0]) * cs_ref[...]
    z_ref[...] = acc.astype(BF16)

    @pl.when(j == TILE_CKV)
    def _():
        zf_ref[...] = acc[:, AF_LANE0:AF_LANE0 + LANES]


def in_projection(x2d, mod_l, w_in_p, b_in_p, col_scale, layer, batch, seq):
    t, d = x2d.shape
    tm = min(1024, seq)
    tiles_per_batch = seq // tm
    col = lambda rows: pl.BlockSpec((1, rows, TN), lambda i, j: (layer, 0, j))
    return pl.pallas_call(
        _in_proj_kernel,
        grid=(t // tm, Z_TILES),
        in_specs=[pl.BlockSpec((tm, d), lambda i, j: (i, 0)),
                  pl.BlockSpec((1, 6, d), lambda i, j: (i // tiles_per_batch, 0, 0)),
                  col(d), col(1), pl.BlockSpec((1, TN), lambda i, j: (0, j))],
        out_specs=[pl.BlockSpec((tm, TN), lambda i, j: (i, j)),
                   pl.BlockSpec((tm, LANES), lambda i, j: (i, 0)),
                   pl.BlockSpec((tm, d), lambda i, j: (i, 0))],
        out_shape=[jax.ShapeDtypeStruct((t, Z_TILES * TN), BF16),
                   jax.ShapeDtypeStruct((t, LANES), F32),
                   jax.ShapeDtypeStruct((t, d), BF16)],
        compiler_params=_cparams(("arbitrary", "arbitrary")),
        name="in_projection",
    )(x2d, mod_l, w_in_p, b_in_p, col_scale)


PROJ_PARTS = 2


def _dil_proj_kernel(h_ref, w_ref, b_ref, cs_ref, *refs, tm, dil, rope):
    if rope:
        rc_ref, rp_ref, rm_ref, o_ref, de_scr = refs
    else:
        o_ref, de_scr = refs
    n_part = PROJ_PARTS
    part_rows = tm // n_part
    n_slab = TN // LANES
    accs = []
    for part in range(n_part):
        rows = slice(part * part_rows, (part + 1) * part_rows)
        accs.append((jnp.dot(h_ref[rows, :], w_ref[0], preferred_element_type=F32) + b_ref[0])
                    * cs_ref[...])
    for part, acc in enumerate(accs):
        rows = slice(part * part_rows, (part + 1) * part_rows)
        slabs = [acc[:, s * LANES:(s + 1) * LANES] for s in range(n_slab)]
        if rope:
            rc, rp, rm = rc_ref[rows, :], rp_ref[rows, :], rm_ref[rows, :]
            slabs = [_rope_slab(x, rc, rp, rm, PARTIAL_ROT // 2) for x in slabs]
        if dil == 1:
            o_ref[0, 0, rows, :] = jnp.concatenate(slabs, axis=1).astype(BF16)
        else:
            n = part_rows // dil
            for s in range(n_slab):
                de_scr[part * n_slab + s] = slabs[s]
            for r in range(dil):
                o_ref[0, r, part * n:(part + 1) * n, :] = jnp.concatenate(
                    [de_scr[part * n_slab + s, pl.ds(r, n, stride=dil), :] for s in range(n_slab)],
                    axis=1).astype(BF16)


def dilated_projection(h, w_in_p, b_in_p, col_scale, tabs, g, layer, batch, seq, first_part, n_parts):
    t, d = h.shape
    dil = DILATION_GROUPS[g][1]
    tm = min(1024, seq)
    tiles_per_batch = seq // tm
    tile = lambda p: TILE_B0 + (first_part + p) * N_DIL + g
    col = lambda rows: pl.BlockSpec((1, rows, TN), lambda i, p: (layer, 0, tile(p)))
    in_specs = [pl.BlockSpec((tm, d), lambda i, p: (i, 0)), col(d), col(1),
                pl.BlockSpec((1, TN), lambda i, p: (0, tile(p)))]
    args = [h, w_in_p, b_in_p, col_scale]
    if tabs is not None:
        in_specs += [pl.BlockSpec((tm, LANES), lambda i, p: (i, 0))] * 3
        args += list(tabs)
    return pl.pallas_call(
        functools.partial(_dil_proj_kernel, tm=tm, dil=dil, rope=tabs is not None),
        grid=(t // tm, n_parts),
        in_specs=in_specs,
        out_specs=pl.BlockSpec((1, dil, tm // dil, TN),
                               lambda i, p: (i // tiles_per_batch, 0, i % tiles_per_batch, p)),
        out_shape=jax.ShapeDtypeStruct((batch, dil, seq // dil, n_parts * TN), BF16),
        scratch_shapes=[pltpu.VMEM((PROJ_PARTS * TN // LANES, tm // PROJ_PARTS, LANES), F32)],
        compiler_params=_cparams(("arbitrary", "arbitrary")),
        name="dilated_projection",
    )(*args)


def _in_segments(d):
    o_af = 3 * W_A
    o_b = o_af + A_HEADS
    o_c = o_b + 3 * N_DIL * W_B
    o_gl = o_c + Q_LORA + KV_LORA + D_ROPE
    c_width = o_gl - o_c
    return ((o_gl, 3 * d, 0), (0, o_af, TILE_AQ * TN), (o_c, c_width, TILE_CQ * TN),
            (o_af, A_HEADS, TILE_CKV * TN + AF_LANE0), (o_b, o_c - o_b, TILE_B0 * TN))


def _pack_kernel(w_ref, o_ref, *, segments):
    o_ref[...] = jnp.zeros_like(o_ref)
    for src, width, dst in segments:
        o_ref[0, :, dst:dst + width] = w_ref[0, :, src:src + width].astype(o_ref.dtype)


def pack_in_weights(w_in, out_dtype, rows):
    depth, d_rows, n_in = w_in.shape
    segments = _in_segments((n_in - (3 * W_A + A_HEADS + 3 * N_DIL * W_B + Q_LORA + KV_LORA + D_ROPE))
                            // 3)
    return pl.pallas_call(
        functools.partial(_pack_kernel, segments=segments),
        grid=(depth, d_rows // rows),
        in_specs=[pl.BlockSpec((1, rows, n_in), lambda l, i: (l, i, 0))],
        out_specs=pl.BlockSpec((1, rows, N_TILES * TN), lambda l, i: (l, i, 0)),
        out_shape=jax.ShapeDtypeStruct((depth, d_rows, N_TILES * TN), out_dtype),
        compiler_params=_cparams(("arbitrary", "arbitrary")),
        name="pack_in_weights",
    )(w_in)


def _logf_kernel(zf_ref, lf_ref, carry):
    @pl.when(pl.program_id(1) == 0)
    def _():
        carry[...] = jnp.zeros_like(carry)

    x = zf_ref[...]
    ls = jnp.minimum(x, 0.0) - jnp.log1p(jnp.exp(-jnp.abs(x)))
    ts = x.shape[0]
    row = lax.broadcasted_iota(jnp.int32, (ts, ts), 0)
    col = lax.broadcasted_iota(jnp.int32, (ts, ts), 1)
    tri = (row >= col).astype(F32)
    cs = jnp.dot(tri, ls, precision=HIGHEST, preferred_element_type=F32) + carry[0:1, :]
    lf_ref[...] = cs * LOG2E
    carry[...] = jnp.broadcast_to(cs[ts - 1:ts, :], carry.shape)


def log_forget_cumsum(zf, batch, seq):
    ts = min(256, seq)
    nt = seq // ts
    return pl.pallas_call(
        _logf_kernel,
        grid=(batch, nt),
        in_specs=[pl.BlockSpec((ts, LANES), lambda b, i: (b * nt + i, 0))],
        out_specs=pl.BlockSpec((ts, LANES), lambda b, i: (b * nt + i, 0)),
        out_shape=jax.ShapeDtypeStruct(zf.shape, F32),
        scratch_shapes=[pltpu.VMEM((SUBLANES, LANES), F32)],
        compiler_params=_cparams(("arbitrary", "arbitrary")),
        name="log_forget_cumsum",
    )(zf)


FLASH_TQ = 2048
FLASH_SUB = 512
V_PAD = 16


def _split3(x):
    hi = x.astype(BF16)
    r1 = x - hi.astype(F32)
    mid = r1.astype(BF16)
    lo = (r1 - mid.astype(F32)).astype(BF16)
    return hi, mid, lo


def _flash_kernel(*refs, decay, dv, seq):
    if decay:
        q_ref, k_ref, v_ref, lfq_ref, lfk_ref, o_ref, kext, vext, m_scr, acc_scr = refs
    else:
        q_ref, k_ref, v_ref, o_ref, vext, m_scr, acc_scr = refs
    qi = pl.program_id(2)
    tq = q_ref.shape[0]
    sub = min(FLASH_SUB, tq)
    n_sub = tq // sub
    dn = (((1,), (1,)), ((), ()))

    @pl.when(qi == 0)
    def _():
        lane = lax.broadcasted_iota(jnp.int32, (seq, LANES), 1)
        for c in range(seq // sub):
            cols = slice(c * sub, (c + 1) * sub)
            vext[0:dv, cols] = v_ref[cols, :].astype(F32).T.astype(BF16)
        ones_row = lax.broadcasted_iota(jnp.int32, (V_PAD, seq), 0) == 0
        vext[dv:, :] = jnp.where(ones_row, 1.0, 0.0).astype(BF16)
        if decay:
            hi, mid, lo = _split3(lfk_ref[0, 0])
            kext[:, :HEAD_DIM] = k_ref[...]
            kext[:, HEAD_DIM:] = jnp.where(
                lane == 0, -hi, jnp.where(lane == 1, -mid, jnp.where(
                    lane == 2, -lo, jnp.where(lane < 6, 1.0, 0.0).astype(BF16))))

    if decay:
        lane = lax.broadcasted_iota(jnp.int32, (tq, LANES), 1)
        hi, mid, lo = _split3(lfq_ref[0, 0])
        q_tail = jnp.where(lane == 3, hi, jnp.where(lane == 4, mid, jnp.where(
            lane == 5, lo, jnp.where(lane < 3, 1.0, 0.0).astype(BF16))))
        q_all = jnp.concatenate([q_ref[...], q_tail], axis=1)
        keys = kext
    else:
        q_all = q_ref[...]
        keys = k_ref

    m_scr[...] = jnp.full_like(m_scr, -jnp.inf)
    acc_scr[...] = jnp.zeros_like(acc_scr)

    def update(work):
        cols = [slice(r * sub, (r + 1) * sub) for r, _, _ in work]
        starts = [pl.multiple_of(c * sub, sub) for _, c, _ in work]
        scores = []
        for (r, c, masked), cl, st in zip(work, cols, starts):
            s = lax.dot_general(keys[pl.ds(st, sub), :], q_all[cl], dn, preferred_element_type=F32)
            if masked:
                key = lax.broadcasted_iota(jnp.int32, s.shape, 0)
                qry = lax.broadcasted_iota(jnp.int32, s.shape, 1)
                s = jnp.where(key <= qry, s, -jnp.inf)
            scores.append(s)
        m_prev = [m_scr[0:1, cl] for cl in cols]
        m_new = [jnp.maximum(mp, jnp.max(s, axis=0, keepdims=True)) for mp, s in zip(m_prev, scores)]
        probs = [jnp.exp2(s - mn).astype(BF16) for s, mn in zip(scores, m_new)]
        for cl, st, mp, mn, p in zip(cols, starts, m_prev, m_new, probs):
            acc_scr[:, cl] = (jnp.exp2(mp - mn) * acc_scr[:, cl]
                              + jnp.dot(vext[:, pl.ds(st, sub)], p, preferred_element_type=F32))
            m_scr[:, cl] = jnp.broadcast_to(mn, (SUBLANES, sub))

    def full_chunks(c, carry):
        update([(r, c, False) for r in range(n_sub)])
        return carry

    lax.fori_loop(0, qi * n_sub, full_chunks, 0)
    for d in range(n_sub):
        update([(r, qi * n_sub + d, r == d) for r in range(d, n_sub)])
    for r in range(n_sub):
        cols = slice(r * sub, (r + 1) * sub)
        o_t = acc_scr[0:dv, cols] / acc_scr[dv:dv + 1, cols]
        o_ref[cols, :] = o_t.T.astype(o_ref.dtype)


def flash_attention(q_arr, k_arr, v_arr, q_col0, k_col0, v_col0, dq, dv, heads, batch, seq,
                    decay=None):
    tq = min(FLASH_TQ, seq)
    nq = seq // tq
    in_specs = [
        pl.BlockSpec((tq, dq), lambda b, h, i: (b * nq + i, q_col0 + h)),
        pl.BlockSpec((seq, dq), lambda b, h, i: (b, k_col0 + h)),
        pl.BlockSpec((seq, dv), lambda b, h, i: (b, v_col0 + h)),
    ]
    args = [q_arr, k_arr, v_arr]
    scratch = [pltpu.VMEM((dv + V_PAD, seq), BF16), pltpu.VMEM((SUBLANES, tq), F32),
               pltpu.VMEM((dv + V_PAD, tq), F32)]
    if decay is not None:
        in_specs += [pl.BlockSpec((1, 1, tq, 1), lambda b, h, i: (b, h, i, 0)),
                     pl.BlockSpec((1, 1, seq, 1), lambda b, h, i: (b, h, 0, 0))]
        args += [decay, decay]
        scratch = [pltpu.VMEM((seq, dq + LANES), BF16)] + scratch
    return pl.pallas_call(
        functools.partial(_flash_kernel, decay=decay is not None, dv=dv, seq=seq),
        grid=(batch, heads, nq),
        in_specs=in_specs,
        out_specs=pl.BlockSpec((tq, dv), lambda b, h, i: (b * nq + i, h)),
        out_shape=jax.ShapeDtypeStruct((batch * seq, heads * dv), BF16),
        scratch_shapes=scratch,
        compiler_params=_cparams(("arbitrary", "arbitrary", "arbitrary")),
        name="flash_fox" if decay is not None else "flash_mla",
    )(*args)


DIL_ROWS = 1024


def _dilated_kernel(q_ref, k_ref, v_ref, o_ref, lse_ref, *, rows):
    c = pl.program_id(2)
    wb = WIN_BLOCK
    row = lax.broadcasted_iota(jnp.int32, (wb, wb), 0)
    col = lax.broadcasted_iota(jnp.int32, (wb, wb), 1)
    cur_ok = col <= row
    dn = (((1,), (1,)), ((), ()))
    chains = []
    for n in range(rows // wb):
        base = c * rows + n * wb
        cur = pl.ds(pl.multiple_of(base, wb), wb)
        prev = pl.ds(pl.multiple_of(jnp.maximum(base - wb, 0), wb), wb)
        prev_ok = (col >= row) & (base > 0)
        for h in range(B_HEADS):
            chains.append((slice(n * wb, (n + 1) * wb), slice(h * HEAD_DIM, (h + 1) * HEAD_DIM),
                           cur, prev, prev_ok))
    scores = []
    for qrows, sl, cur, prev, prev_ok in chains:
        q = q_ref[0, 0, qrows, sl]
        s_c = lax.dot_general(q, k_ref[0, 0, cur, sl], dn, preferred_element_type=F32)
        s_p = lax.dot_general(q, k_ref[0, 0, prev, sl], dn, preferred_element_type=F32)
        scores.append((jnp.where(cur_ok, s_c, -jnp.inf), jnp.where(prev_ok, s_p, -jnp.inf)))
    maxes = [jnp.maximum(jnp.max(s_c, axis=1, keepdims=True), jnp.max(s_p, axis=1, keepdims=True))
             for s_c, s_p in scores]
    probs = [(jnp.exp(s_c - m), jnp.exp(s_p - m)) for (s_c, s_p), m in zip(scores, maxes)]
    sums = [jnp.sum(p_c, axis=1, keepdims=True) + jnp.sum(p_p, axis=1, keepdims=True)
            for p_c, p_p in probs]
    for (qrows, sl, cur, prev, _), (p_c, p_p), m, l in zip(chains, probs, maxes, sums):
        o = (jnp.dot(p_c.astype(BF16), v_ref[0, 0, cur, sl], preferred_element_type=F32)
             + jnp.dot(p_p.astype(BF16), v_ref[0, 0, prev, sl], preferred_element_type=F32))
        o_ref[0, 0, qrows, sl] = (o / l).astype(BF16)
        lse_ref[0, 0, qrows, sl] = jnp.broadcast_to(m + jnp.log(l), (wb, HEAD_DIM))


def dilated_attention(zqk, zv, batch):
    _, dil, l_sub, _ = zv.shape
    rows = min(DIL_ROWS, l_sub)
    chunk = lambda c: pl.BlockSpec((1, 1, rows, W_B), lambda b, r, i: (b, r, i, c))
    whole = lambda c: pl.BlockSpec((1, 1, l_sub, W_B), lambda b, r, i: (b, r, 0, c))
    return pl.pallas_call(
        functools.partial(_dilated_kernel, rows=rows),
        grid=(batch, dil, l_sub // rows),
        in_specs=[chunk(0), whole(1), whole(0)],
        out_specs=[chunk(0), chunk(0)],
        out_shape=[jax.ShapeDtypeStruct((batch, dil, l_sub, W_B), BF16),
                   jax.ShapeDtypeStruct((batch, dil, l_sub, W_B), F32)],
        compiler_params=_cparams(("arbitrary", "arbitrary", "arbitrary")),
        name="dilated_attention",
    )(zqk, zqk, zv)


def _rms(x, g):
    return x * lax.rsqrt(jnp.mean(x * x, axis=-1, keepdims=True) + LN_EPS) * g


def _mla_up_kernel(cq_ref, ckv_ref, gq_ref, gkv_ref, wq_ref, wk_ref, wv_ref, rc_ref, rp_ref, rm_ref,
                   q_ref, k_ref, v_ref):
    scale = (D_NOPE + D_ROPE) ** -0.5 * LOG2E
    half = D_ROPE // 2
    rc, rp, rm = rc_ref[...], rp_ref[...], rm_ref[...]
    cqn = _rms(cq_ref[...].astype(F32), gq_ref[...]).astype(BF16)
    q = jnp.dot(cqn, wq_ref[...], preferred_element_type=F32) * scale
    ckvr = ckv_ref[...].astype(F32)
    ckvn = _rms(ckvr[:, :KV_LORA], gkv_ref[...]).astype(BF16)
    k_rope = _rope_slab(ckvr[:, KV_LORA:KV_LORA + LANES], rc, rp, rm, half).astype(BF16)
    k_nope = jnp.dot(ckvn, wk_ref[...], preferred_element_type=F32)
    v_ref[...] = jnp.dot(ckvn, wv_ref[...], preferred_element_type=F32).astype(BF16)
    for h in range(C_HEADS):
        lo = h * 2 * LANES
        q_ref[:, lo:lo + LANES] = q[:, lo:lo + LANES].astype(BF16)
        q_ref[:, lo + LANES:lo + 2 * LANES] = _rope_slab(
            q[:, lo + LANES:lo + 2 * LANES], rc, rp, rm, half).astype(BF16)
        k_ref[:, lo:lo + LANES] = k_nope[:, h * LANES:(h + 1) * LANES].astype(BF16)
        k_ref[:, lo + LANES:lo + 2 * LANES] = k_rope


def mla_up_projection(z, g_cq, g_ckv, wq_p, wk_p, wv_p, tabs_c):
    t = z.shape[0]
    tm = min(512, t)
    rc, rp, rm = tabs_c
    full = lambda a: pl.BlockSpec(a.shape, lambda i: (0,) * a.ndim)
    tab = pl.BlockSpec((tm, LANES), lambda i: (i, 0))
    row = lambda w: pl.BlockSpec((tm, w), lambda i: (i, 0))
    return pl.pallas_call(
        _mla_up_kernel,
        grid=(t // tm,),
        in_specs=[pl.BlockSpec((tm, TN), lambda i: (i, TILE_CQ)),
                  pl.BlockSpec((tm, TN), lambda i: (i, TILE_CKV)),
                  full(g_cq), full(g_ckv), full(wq_p), full(wk_p), full(wv_p), tab, tab, tab],
        out_specs=[row(C_HEADS * 2 * LANES), row(C_HEADS * 2 * LANES), row(W_C)],
        out_shape=[jax.ShapeDtypeStruct((t, C_HEADS * 2 * LANES), BF16),
                   jax.ShapeDtypeStruct((t, C_HEADS * 2 * LANES), BF16),
                   jax.ShapeDtypeStruct((t, W_C), BF16)],
        compiler_params=_cparams(("arbitrary",)),
        name="mla_up_projection",
    )(z, z, g_cq, g_ckv, wq_p, wk_p, wv_p, rc, rp, rm)


def pack_mla_weights(w_uq_l, w_ukv_l):
    wq = w_uq_l.reshape(Q_LORA, C_HEADS, D_NOPE + D_ROPE)
    wq = jnp.pad(wq, ((0, 0), (0, 0), (0, 2 * LANES - D_NOPE - D_ROPE)))
    wkv = w_ukv_l.reshape(KV_LORA, C_HEADS, D_NOPE + D_VC)
    wk = wkv[:, :, :D_NOPE].reshape(KV_LORA, C_HEADS * D_NOPE)
    wv = wkv[:, :, D_NOPE:].reshape(KV_LORA, W_C)
    return (wq.reshape(Q_LORA, C_HEADS * 2 * LANES).astype(BF16), wk.astype(BF16), wv.astype(BF16))


def _merge_kernel(gla_ref, glb_ref, glc_ref, oa_ref, ob0_ref, ob1_ref, ob2_ref,
                  ls0_ref, ls1_ref, ls2_ref, oc_ref, wa_ref, wb_ref, wc_ref, y_ref,
                  o_scr, l_scr, *, tm):
    for g, (o_ref, ls_ref) in enumerate(((ob0_ref, ls0_ref), (ob1_ref, ls1_ref), (ob2_ref, ls2_ref))):
        dil = DILATION_GROUPS[g][1]
        for h in range(B_HEADS):
            sl = slice(h * HEAD_DIM, (h + 1) * HEAD_DIM)
            for r in range(dil):
                rows = pl.ds(r, tm // dil, stride=dil) if dil > 1 else slice(None)
                o_scr[g * B_HEADS + h, rows, :] = o_ref[0, r, :, sl].astype(F32)
                l_scr[g * B_HEADS + h, rows, :] = ls_ref[0, r, :, sl]
    slabs = []
    for h in range(B_HEADS):
        l0, l1, l2 = l_scr[h], l_scr[B_HEADS + h], l_scr[2 * B_HEADS + h]
        m = jnp.maximum(jnp.maximum(l0, l1), l2)
        e0, e1, e2 = jnp.exp(l0 - m), jnp.exp(l1 - m), jnp.exp(l2 - m)
        den = e0 + e1 + e2
        slabs.append((e0 / den) * o_scr[h] + (e1 / den) * o_scr[B_HEADS + h]
                     + (e2 / den) * o_scr[2 * B_HEADS + h])
    o_b = jnp.concatenate(slabs, axis=1).astype(BF16)
    pa = jnp.dot(oa_ref[...], wa_ref[...], preferred_element_type=F32)
    pb = jnp.dot(o_b, wb_ref[...], preferred_element_type=F32)
    pc = jnp.dot(oc_ref[...], wc_ref[...], preferred_element_type=F32)
    y = (jax.nn.sigmoid(gla_ref[...].astype(F32)) * pa
         + jax.nn.sigmoid(glb_ref[...].astype(F32)) * pb
         + jax.nn.sigmoid(glc_ref[...].astype(F32)) * pc)
    y_ref[...] = y.astype(BF16)


def branch_merge(z, o_a, obs, lses, o_c, wa, wb, wc, batch, seq):
    t = z.shape[0]
    d = wa.shape[1]
    tm = min(512, seq)
    tpb = seq // tm
    full = lambda a: pl.BlockSpec(a.shape, lambda i: (0,) * a.ndim)
    gl = lambda c: pl.BlockSpec((tm, d), lambda i: (i, c))
    res = lambda dil: pl.BlockSpec((1, dil, tm // dil, W_B), lambda i: (i // tpb, 0, i % tpb, 0))
    res_specs = [res(dil) for _, dil in DILATION_GROUPS]
    return pl.pallas_call(
        functools.partial(_merge_kernel, tm=tm),
        grid=(t // tm,),
        in_specs=[gl(0), gl(1), gl(2), pl.BlockSpec((tm, W_A), lambda i: (i, 0))]
        + res_specs + res_specs
        + [pl.BlockSpec((tm, W_C), lambda i: (i, 0)), full(wa), full(wb), full(wc)],
        out_specs=pl.BlockSpec((tm, d), lambda i: (i, 0)),
        out_shape=jax.ShapeDtypeStruct((t, d), BF16),
        scratch_shapes=[pltpu.VMEM((N_DIL * B_HEADS, tm, HEAD_DIM), F32),
                        pltpu.VMEM((N_DIL * B_HEADS, tm, HEAD_DIM), F32)],
        compiler_params=_cparams(("arbitrary",)),
        name="branch_merge",
    )(z, z, z, o_a, *obs, *lses, o_c, wa, wb, wc)


def _layer_norm(x, g, b):
    mu = jnp.mean(x, axis=-1, keepdims=True)
    xc = x - mu
    var = jnp.mean(xc * xc, axis=-1, keepdims=True)
    return xc * lax.rsqrt(var + LN_EPS) * g + b


def _out_router_kernel(y_ref, x_ref, mod_ref, wo_ref, lng_ref, lnb_ref, wr_ref, br_ref,
                       x1_ref, h2_ref, ri_ref, rf_ref, cnt_ref, carry, *, tm, alpha):
    @pl.when(pl.program_id(0) == 0)
    def _():
        carry[...] = jnp.zeros_like(carry)

    gate1 = mod_ref[0, 2:3, :]
    shift2 = mod_ref[0, 3:4, :]
    scale2 = mod_ref[0, 4:5, :]
    mix = jnp.dot(y_ref[...], wo_ref[...], preferred_element_type=F32)
    x1 = _layer_norm(alpha * x_ref[...] + (1.0 + gate1) * mix, lng_ref[...], lnb_ref[...])
    x1_ref[...] = x1
    h2 = x1 * (1.0 + scale2) + shift2
    d = h2.shape[1]
    for c in range(d // LANES):
        h2_ref[pl.ds(c, tm, stride=d // LANES), :] = h2[:, c * LANES:(c + 1) * LANES]

    h_hi = h2.astype(BF16)
    h_lo = (h2 - h_hi.astype(F32)).astype(BF16)
    logits = (jnp.dot(h_hi, wr_ref[0], preferred_element_type=F32)
              + jnp.dot(h_lo, wr_ref[0], preferred_element_type=F32)
              + jnp.dot(h_hi, wr_ref[1], preferred_element_type=F32))
    s = jax.nn.sigmoid(logits.T[:N_EXPERTS])
    sel = s + br_ref[...]
    sel_m = [sel[i * N_GROUPS:(i + 1) * N_GROUPS] for i in range(EXPERTS_PER_GROUP)]
    s_m = [s[i * N_GROUPS:(i + 1) * N_GROUPS] for i in range(EXPERTS_PER_GROUP)]
    gscore = None
    for a in range(EXPERTS_PER_GROUP):
        for b in range(a + 1, EXPERTS_PER_GROUP):
            pair = sel_m[a] + sel_m[b]
            gscore = pair if gscore is None else jnp.maximum(gscore, pair)
    grp = lax.broadcasted_iota(jnp.int32, (N_GROUPS, tm), 0).astype(F32)
    gmax = jnp.max(gscore, axis=0, keepdims=True)
    gbest = jnp.min(jnp.where(gscore == gmax, grp, float(N_GROUPS)), axis=0, keepdims=True)
    in_best = grp == gbest
    v = [jnp.sum(jnp.where(in_best, m, 0.0), axis=0, keepdims=True) for m in sel_m]
    u = [jnp.sum(jnp.where(in_best, m, 0.0), axis=0, keepdims=True) for m in s_m]

    def first_argmax(vals):
        best = vals[0]
        for x in vals[1:]:
            best = jnp.maximum(best, x)
        idx = jnp.full_like(gbest, float(EXPERTS_PER_GROUP - 1))
        for i in range(EXPERTS_PER_GROUP - 2, -1, -1):
            idx = jnp.where(vals[i] == best, float(i), idx)
        return idx

    i1 = first_argmax(v)
    i2 = first_argmax([jnp.where(i1 == i, -jnp.inf, v[i]) for i in range(EXPERTS_PER_GROUP)])

    def member(vals, idx):
        out = vals[EXPERTS_PER_GROUP - 1]
        for i in range(EXPERTS_PER_GROUP - 2, -1, -1):
            out = jnp.where(idx == i, vals[i], out)
        return out

    u1, u2 = member(u, i1), member(u, i2)
    w1 = u1 / (u1 + u2)
    w2 = u2 / (u1 + u2)
    e1 = gbest * EXPERTS_PER_GROUP + i1
    e2 = gbest * EXPERTS_PER_GROUP + i2
    row1 = i1 * N_GROUPS + gbest
    row2 = i2 * N_GROUPS + gbest

    erow = lax.broadcasted_iota(jnp.int32, (N_EXPERTS, tm), 0).astype(F32)
    hit1, hit2 = erow == row1, erow == row2
    onehot = (hit1 | hit2).astype(F32)
    t_row = lax.broadcasted_iota(jnp.int32, (tm, tm), 0)
    t_col = lax.broadcasted_iota(jnp.int32, (tm, tm), 1)
    earlier = (t_row < t_col).astype(BF16)
    cnt = jnp.dot(onehot.astype(BF16), earlier, preferred_element_type=F32) + carry[:, 0:1]
    r1 = jnp.sum(jnp.where(hit1, cnt, 0.0), axis=0, keepdims=True)
    r2 = jnp.sum(jnp.where(hit2, cnt, 0.0), axis=0, keepdims=True)
    new_carry = carry[...] + jnp.sum(onehot, axis=1, keepdims=True)
    carry[...] = new_carry
    cnt_ref[...] = new_carry

    sub = lax.broadcasted_iota(jnp.int32, (SUBLANES, tm), 0)
    ri = jnp.where(sub == 0, e1, jnp.where(sub == 1, e2, jnp.where(sub == 2, r1, jnp.where(
        sub == 3, r2, 0.0))))
    ri_ref[0] = ri.astype(jnp.int32)
    rf_ref[0] = jnp.where(sub == 0, w1, jnp.where(sub == 1, w2, 0.0))


def out_proj_router(y, x2d, mod_l, w_o, ln_g, ln_b, wr_p, br_p, batch, seq, alpha):
    t, d = x2d.shape
    tm = min(256, seq)
    tpb = seq // tm
    nt = t // tm
    full = lambda a: pl.BlockSpec(a.shape, lambda i: (0,) * a.ndim)
    tile = pl.BlockSpec((tm, d), lambda i: (i, 0))
    rows = pl.BlockSpec((1, SUBLANES, tm), lambda i: (i, 0, 0))
    rt = d // LANES
    return pl.pallas_call(
        functools.partial(_out_router_kernel, tm=tm, alpha=alpha),
        grid=(nt,),
        in_specs=[tile, tile, pl.BlockSpec((1, 6, d), lambda i: (i // tpb, 0, 0)),
                  full(w_o), full(ln_g), full(ln_b), full(wr_p), full(br_p)],
        out_specs=[tile, pl.BlockSpec((tm * rt, LANES), lambda i: (i, 0)), rows, rows,
                   pl.BlockSpec((N_EXPERTS, LANES), lambda i: (0, 0))],
        out_shape=[jax.ShapeDtypeStruct((t, d), F32),
                   jax.ShapeDtypeStruct((t * rt, LANES), F32),
                   jax.ShapeDtypeStruct((nt, SUBLANES, tm), jnp.int32),
                   jax.ShapeDtypeStruct((nt, SUBLANES, tm), F32),
                   jax.ShapeDtypeStruct((N_EXPERTS, LANES), F32)],
        scratch_shapes=[pltpu.VMEM((N_EXPERTS, LANES), F32)],
        compiler_params=_cparams(("arbitrary",)),
        name="out_proj_router",
    )(y, x2d, mod_l, w_o, ln_g, ln_b, wr_p, br_p)


def _dispatch_kernel(dest_ref, h_ref, xs_in_ref, xs_ref, sem, *, tm, rt):
    del xs_in_ref

    def row_copy(a, t):
        return pltpu.make_async_copy(h_ref.at[pl.ds(t * rt, rt), :], xs_ref.at[dest_ref[0, 0, a]], sem)

    def issue(t, carry):
        row_copy(t, t).start()
        row_copy(tm + t, t).start()
        return carry

    def drain(t, carry):
        row_copy(t, t).wait()
        row_copy(tm + t, t).wait()
        return carry

    lax.fori_loop(0, tm, issue, 0, unroll=8)
    lax.fori_loop(0, tm, drain, 0, unroll=8)


def moe_dispatch(dest_tiles, h2t, xs_buf, tm):
    nt = dest_tiles.shape[0]
    rt = h2t.shape[0] // (nt * tm)
    return pl.pallas_call(
        functools.partial(_dispatch_kernel, tm=tm, rt=rt),
        grid=(nt,),
        in_specs=[pl.BlockSpec((1, 1, 2 * tm), lambda i: (i, 0, 0), memory_space=pltpu.SMEM),
                  pl.BlockSpec((tm * rt, LANES), lambda i: (i, 0)),
                  pl.BlockSpec(memory_space=pl.ANY)],
        out_specs=pl.BlockSpec(memory_space=pl.ANY),
        out_shape=jax.ShapeDtypeStruct(xs_buf.shape, xs_buf.dtype),
        scratch_shapes=[pltpu.SemaphoreType.DMA(())],
        input_output_aliases={2: 0},
        compiler_params=_cparams(("arbitrary",)),
        name="moe_dispatch",
    )(dest_tiles, h2t, xs_buf)


def _expert_kernel(be_ref, nu_ref, xs_ref, wg_ref, wu_ref, wd_ref, ys_ref, wg_s, wu_s, wd_s, *, rt):
    n = pl.program_id(0)

    @pl.when(n < nu_ref[0])
    def _():
        prev = be_ref[jnp.maximum(n - 1, 0)]

        @pl.when((n == 0) | (be_ref[n] != prev))
        def _():
            wg_s[...] = wg_ref[0, 0].astype(BF16)
            wu_s[...] = wu_ref[0, 0].astype(BF16)
            wd_s[...] = wd_ref[0, 0].astype(BF16)

        n_part = MOE_PARTS
        pr = MOE_BLOCK // n_part
        xs = [jnp.concatenate([xs_ref[pl.ds(p * pr * rt + c, pr, stride=rt), :] for c in range(rt)],
                              axis=1).astype(BF16) for p in range(n_part)]
        gates = [jnp.dot(x, wg_s[...], preferred_element_type=F32) for x in xs]
        ups = [jnp.dot(x, wu_s[...], preferred_element_type=F32) for x in xs]
        acts = [(g * jax.nn.sigmoid(g) * u).astype(BF16) for g, u in zip(gates, ups)]
        ys = [jnp.dot(a, wd_s[...], preferred_element_type=F32) for a in acts]
        for p, y in enumerate(ys):
            for c in range(rt):
                ys_ref[pl.ds(p * pr * rt + c, pr, stride=rt), :] = y[:, c * LANES:(c + 1) * LANES]

    @pl.when(n >= nu_ref[0])
    def _():
        ys_ref[...] = jnp.zeros_like(ys_ref)


def expert_ffn(blk_e, n_used, xs2d, w_gate, w_up, w_down, layer):
    depth, n_exp, d, de = w_gate.shape
    rt = d // LANES
    n_blk = xs2d.shape[0] // (MOE_BLOCK * rt)
    rows = lambda n, be, nu: (jnp.minimum(n, nu[0] - 1), 0)
    grid_spec = pltpu.PrefetchScalarGridSpec(
        num_scalar_prefetch=2,
        grid=(n_blk,),
        in_specs=[pl.BlockSpec((MOE_BLOCK * rt, LANES), rows),
                  pl.BlockSpec((1, 1, d, de), lambda n, be, nu: (layer, be[n], 0, 0)),
                  pl.BlockSpec((1, 1, d, de), lambda n, be, nu: (layer, be[n], 0, 0)),
                  pl.BlockSpec((1, 1, de, d), lambda n, be, nu: (layer, be[n], 0, 0))],
        out_specs=pl.BlockSpec((MOE_BLOCK * rt, LANES), lambda n, be, nu: (n, 0)),
        scratch_shapes=[pltpu.VMEM((d, de), BF16), pltpu.VMEM((d, de), BF16),
                        pltpu.VMEM((de, d), BF16)])
    return pl.pallas_call(
        functools.partial(_expert_kernel, rt=rt),
        grid_spec=grid_spec,
        out_shape=jax.ShapeDtypeStruct(xs2d.shape, F32),
        compiler_params=_cparams(("arbitrary",)),
        name="expert_ffn",
    )(blk_e, n_used, xs2d, w_gate, w_up, w_down)


def _combine_kernel(dest_ref, dnext_ref, ys_ref, rf_ref, x_ref, mod_ref, lng_ref, lnb_ref, o_ref,
                    buf, sem, *, tm, rt, alpha, nt):
    i = pl.program_id(0)
    slot = i % 2

    def row_copy(dref, a, sl):
        return pltpu.make_async_copy(ys_ref.at[dref[0, 0, a]], buf.at[sl, pl.ds(a * rt, rt), :],
                                     sem.at[sl])

    def issue(dref, sl):
        def body(a, carry):
            row_copy(dref, a, sl).start()
            return carry
        lax.fori_loop(0, 2 * tm, body, 0, unroll=8)

    @pl.when(i == 0)
    def _():
        issue(dest_ref, 0)

    @pl.when(i + 1 < nt)
    def _():
        issue(dnext_ref, 1 - slot)

    def drain(a, carry):
        row_copy(dest_ref, a, slot).wait()
        return carry

    lax.fori_loop(0, 2 * tm, drain, 0, unroll=8)
    cur = buf.at[slot]

    def rows(k):
        return jnp.concatenate(
            [cur[pl.ds(k * tm * rt + c, tm, stride=rt), :] for c in range(rt)], axis=1)

    ff = rf_ref[:, 0:1] * rows(0) + rf_ref[:, 1:2] * rows(1)
    gate2 = mod_ref[0, 5:6, :]
    o_ref[...] = _layer_norm(alpha * x_ref[...] + (1.0 + gate2) * ff, lng_ref[...], lnb_ref[...])


def moe_combine(dest_tiles, ys3d, rf, x1, mod_l, ln_g, ln_b, batch, seq, tm, alpha):
    t, d = x1.shape
    rt = d // LANES
    tpb = seq // tm
    full = lambda a: pl.BlockSpec(a.shape, lambda i: (0,) * a.ndim)
    tile = pl.BlockSpec((tm, d), lambda i: (i, 0))
    nt = t // tm
    return pl.pallas_call(
        functools.partial(_combine_kernel, tm=tm, rt=rt, alpha=alpha, nt=nt),
        grid=(nt,),
        in_specs=[pl.BlockSpec((1, 1, 2 * tm), lambda i: (i, 0, 0), memory_space=pltpu.SMEM),
                  pl.BlockSpec((1, 1, 2 * tm), lambda i: (jnp.minimum(i + 1, nt - 1), 0, 0),
                               memory_space=pltpu.SMEM),
                  pl.BlockSpec(memory_space=pl.ANY),
                  pl.BlockSpec((tm, TOP_K), lambda i: (i, 0)),
                  tile, pl.BlockSpec((1, 6, d), lambda i: (i // tpb, 0, 0)), full(ln_g), full(ln_b)],
        out_specs=tile,
        out_shape=jax.ShapeDtypeStruct((t, d), F32),
        scratch_shapes=[pltpu.VMEM((2, 2 * tm * rt, LANES), F32), pltpu.SemaphoreType.DMA((2,))],
        compiler_params=_cparams(("arbitrary",)),
        name="moe_combine",
    )(dest_tiles, dest_tiles, ys3d, rf, x1, mod_l, ln_g, ln_b)


def _routing_tables(ri, cnt, n_rows, tm):
    nt = ri.shape[0]
    counts = cnt[:, 0].astype(jnp.int32)
    counts = counts.reshape(EXPERTS_PER_GROUP, N_GROUPS).T.reshape(N_EXPERTS)
    padded = (counts + MOE_BLOCK - 1) // MOE_BLOCK * MOE_BLOCK
    pend = jnp.cumsum(padded)
    pstart = pend - padded
    e_sel = ri[:, 0:TOP_K, :, None] == jnp.arange(N_EXPERTS, dtype=jnp.int32)
    dest = jnp.sum(jnp.where(e_sel, pstart, 0), axis=-1) + ri[:, TOP_K:2 * TOP_K, :]
    dest_tiles = dest.reshape(nt, 1, TOP_K * tm)
    n_blk = n_rows // MOE_BLOCK
    blk_start = jnp.arange(n_blk, dtype=jnp.int32) * MOE_BLOCK
    blk_e = jnp.minimum(jnp.sum(pend[None, :] <= blk_start[:, None], axis=1), N_EXPERTS - 1)
    n_used = (pend[-1:] // MOE_BLOCK).astype(jnp.int32)
    return dest_tiles.astype(jnp.int32), blk_e.astype(jnp.int32), n_used


def kernel(x, c, positions, w_ada, b_ada, w_in, b_in, g_cq, g_ckv, w_uq, w_ukv, w_pa, w_pb, w_pc, w_o,
           ln1_g, ln1_b, w_router, b_router, w_gate, w_up, w_down, ln2_g, ln2_b):
    batch, seq, d = x.shape
    depth = w_ada.shape[0]
    t = batch * seq
    alpha = (2 * depth) ** 0.25
    rt = d // LANES
    assert rt == ROW_TILE and seq % 2048 == 0 and d % TN == 0
    tm_moe = min(256, seq)

    c_pad = jnp.zeros((SUBLANES, d), F32).at[:batch].set(c)
    mod_all = ada_modulation(c_pad, w_ada, b_ada)[:, :batch].reshape(depth, batch, 6, d)
    pos_col = positions.reshape(t, 1)
    tabs_b = rope_tables(pos_col, PARTIAL_ROT)
    tabs_c = rope_tables(pos_col, D_ROPE)
    qk_scale = HEAD_DIM ** -0.5
    col_scale = jnp.ones((1, N_TILES * TN), F32)
    col_scale = col_scale.at[:, TILE_AQ * TN:(TILE_AQ + 1) * TN].set(qk_scale * LOG2E)
    col_scale = col_scale.at[:, TILE_B0 * TN:(TILE_B0 + N_DIL) * TN].set(qk_scale)

    perm = np.arange(N_EXPERTS).reshape(N_GROUPS, EXPERTS_PER_GROUP).T.reshape(-1)
    wr_f = jnp.zeros((d, LANES), F32).at[:, :N_EXPERTS].set(w_router.astype(F32)[:, perm])
    wr_hi = wr_f.astype(BF16)
    wr_p = jnp.stack([wr_hi, (wr_f - wr_hi.astype(F32)).astype(BF16)])
    br_p = b_router.astype(F32)[perm][:, None]

    n_rows = -(-(t * TOP_K + N_EXPERTS * (MOE_BLOCK - 1)) // MOE_BLOCK) * MOE_BLOCK
    xs_buf = jnp.zeros((n_rows, rt, LANES), F32)

    w_in_p = pack_in_weights(w_in, BF16, rows=128)
    b_in_p = pack_in_weights(b_in[:, None, :], F32, rows=1)

    x2d = x.reshape(t, d)
    for l in range(depth):
        z, zf, h = in_projection(x2d, mod_all[l], w_in_p, b_in_p, col_scale, l, batch, seq)
        zqks = [dilated_projection(h, w_in_p, b_in_p, col_scale, tabs_b, g, l, batch, seq, 0, 2)
                for g in range(N_DIL)]
        zvs = [dilated_projection(h, w_in_p, b_in_p, col_scale, None, g, l, batch, seq, 2, 1)
               for g in range(N_DIL)]

        lf = log_forget_cumsum(zf, batch, seq)[:, :A_HEADS].reshape(batch, seq, A_HEADS)
        lf = lf.transpose(0, 2, 1)[:, :, :, None]
        o_a = flash_attention(z, z, z, TILE_AQ * 4, TILE_AQ * 4 + 4, TILE_AQ * 4 + 8, HEAD_DIM,
                              HEAD_DIM, A_HEADS, batch, seq, decay=lf)

        obs, lses = [], []
        for zqk, zv in zip(zqks, zvs):
            o_g, lse_g = dilated_attention(zqk, zv, batch)
            obs.append(o_g)
            lses.append(lse_g)

        wq_p, wk_p, wv_p = pack_mla_weights(w_uq[l], w_ukv[l])
        q_c, k_c, v_c = mla_up_projection(z, g_cq[l][None, :], g_ckv[l][None, :], wq_p, wk_p, wv_p,
                                          tabs_c)
        o_c = flash_attention(q_c, k_c, v_c, 0, 0, 0, 2 * LANES, D_VC, C_HEADS, batch, seq)

        y = branch_merge(z, o_a, obs, lses, o_c, w_pa[l].astype(BF16), w_pb[l].astype(BF16),
                         w_pc[l].astype(BF16), batch, seq)
        x1, h2t, ri, rf, cnt = out_proj_router(y, x2d, mod_all[l], w_o[l].astype(BF16),
                                               ln1_g[l][None, :], ln1_b[l][None, :], wr_p, br_p,
                                               batch, seq, alpha)

        dest_tiles, blk_e, n_used = _routing_tables(ri, cnt, n_rows, tm_moe)
        wts = rf[:, 0:TOP_K, :].transpose(0, 2, 1).reshape(t, TOP_K)
        xs_buf = moe_dispatch(dest_tiles, h2t, xs_buf, tm_moe)
        ys = expert_ffn(blk_e, n_used, xs_buf.reshape(n_rows * rt, LANES), w_gate, w_up, w_down, l)
        x2d = moe_combine(dest_tiles, ys.reshape(n_rows, rt, LANES), wts, x1, mod_all[l],
                          ln2_g[l][None, :], ln2_b[l][None, :], batch, seq, tm_moe, alpha)
    return x2d.reshape(batch, seq, d)
```

```python
import functools

import jax
import jax.numpy as jnp
import numpy as np
from jax import lax
from jax.experimental import pallas as pl
from jax.experimental.pallas import tpu as pltpu

F32 = jnp.float32
BF16 = jnp.bfloat16
HIGHEST = lax.Precision.HIGHEST

LANES = 128
SUBLANES = 8
VMEM_LIMIT = 56 * 1024 * 1024

HEAD_DIM = 128
A_HEADS = 4
B_HEADS = 4
DILATION_GROUPS = ((128, 1), (512, 4), (2048, 16))
N_DIL = 3
C_HEADS = 6
Q_LORA = 512
KV_LORA = 256
D_NOPE = 128
D_ROPE = 64
D_VC = 128
ROPE_THETA = 500000.0
PARTIAL_ROT = HEAD_DIM // 4
WIN_BLOCK = 128
N_EXPERTS = 32
N_GROUPS = 8
EXPERTS_PER_GROUP = N_EXPERTS // N_GROUPS
TOP_K = 2
LN_EPS = 1e-5
LOG2E = 1.4426950408889634
W_A =A_HEADS * HEAD_DIM
W_B = B_HEADS * HEAD_DIM
W_C = C_HEADS * D_VC

TN = 512
TILE_AQ = 12
TILE_CQ = 15
TILE_CKV = 16
TILE_B0 = 17
N_TILES = 26
Z_TILES = 17
AF_LANE0 = KV_LORA + LANES
ROW_TILE = 16
MOE_BLOCK = 512
MOE_PARTS = 2


def _cparams(sem):
    return pltpu.CompilerParams(dimension_semantics=sem, vmem_limit_bytes=VMEM_LIMIT)


def _ada_kernel(c_ref, w_ref, b_ref, o_ref):
    o_ref[0] = jnp.dot(c_ref[...], w_ref[0], precision=HIGHEST,
                       preferred_element_type=F32) + b_ref[0]


def ada_modulation(c_pad, w_ada, b_ada):
    depth, d, n = w_ada.shape
    tn = 1024
    return pl.pallas_call(
        _ada_kernel,
        grid=(depth, n // tn),
        in_specs=[pl.BlockSpec((SUBLANES, d), lambda l, j: (0, 0)),
                  pl.BlockSpec((1, d, tn), lambda l, j: (l, 0, j)),
                  pl.BlockSpec((1, 1, tn), lambda l, j: (l, 0, j))],
        out_specs=pl.BlockSpec((1, SUBLANES, tn), lambda l, j: (l, 0, j)),
        out_shape=jax.ShapeDtypeStruct((depth, SUBLANES, n), F32),
        compiler_params=_cparams(("arbitrary", "arbitrary")),
        name="ada_modulation",
    )(c_pad, w_ada, b_ada.reshape(depth, 1, n))


def _rope_table_kernel(pos_ref, invf_ref, mhi_ref, mlo_ref, c_ref, sp_ref, sm_ref):
    ang = pos_ref[...].astype(F32) * invf_ref[...]
    sin = jnp.sin(ang)
    c_ref[...] = jnp.cos(ang)
    sp_ref[...] = sin * mhi_ref[...]
    sm_ref[...] = -sin * mlo_ref[...]


def rope_tables(pos_col, rot_dim):
    t = pos_col.shape[0]
    half = rot_dim // 2
    inv_freq = 1.0 / (ROPE_THETA ** (jnp.arange(0, rot_dim, 2, dtype=F32) / rot_dim))
    lane = np.arange(LANES)
    invf = jnp.where(lane < rot_dim, jnp.tile(inv_freq, LANES // half), 0.0).astype(F32)[None, :]
    mhi = jnp.asarray(((lane >= half) & (lane < rot_dim)).astype(np.float32))[None, :]
    mlo = jnp.asarray((lane < half).astype(np.float32))[None, :]
    tt = min(t, 1024)
    row = pl.BlockSpec((1, LANES), lambda i: (0, 0))
    out = pl.BlockSpec((tt, LANES), lambda i: (i, 0))
    return pl.pallas_call(
        _rope_table_kernel,
        grid=(t // tt,),
        in_specs=[pl.BlockSpec((tt, 1), lambda i: (i, 0)), row, row, row],
        out_specs=[out, out, out],
        out_shape=[jax.ShapeDtypeStruct((t, LANES), F32)] * 3,
        compiler_params=_cparams(("arbitrary",)),
        name="rope_tables",
    )(pos_col, invf, mhi, mlo)


def _rope_slab(x, c, sp, sm, half):
    return x * c + pltpu.roll(x, half, 1) * sp + pltpu.roll(x, LANES - half, 1) * sm


def _in_proj_kernel(x_ref, mod_ref, w_ref, b_ref, cs_ref, z_ref, zf_ref, h_ref):
    j = pl.program_id(1)

    @pl.when(j == 0)
    def _():
        shift = mod_ref[0, 0:1, :]
        scale = mod_ref[0, 1:2, :]
        h_ref[...] = (x_ref[...] * (1.0 + scale) + shift).astype(BF16)

    acc = (jnp.dot(h_ref[...], w_ref[0], preferred_element_type=F32) + b_ref[0]) * cs_ref[...]
    z_ref[...] = acc.astype(BF16)

    @pl.when(j == TILE_CKV)
    def _():
        zf_ref[...] = acc[:, AF_LANE0:AF_LANE0 + LANES]


def in_projection(x2d, mod_l, w_in_p, b_in_p, col_scale, layer, batch, seq):
    t, d = x2d.shape
    tm = min(1024, seq)
    tiles_per_batch = seq // tm
    col = lambda rows: pl.BlockSpec((1, rows, TN), lambda i, j: (layer, 0, j))
    return pl.pallas_call(
        _in_proj_kernel,
        grid=(t // tm, Z_TILES),
        in_specs=[pl.BlockSpec((tm, d), lambda i, j: (i, 0)),
                  pl.BlockSpec((1, 6, d), lambda i, j: (i // tiles_per_batch, 0, 0)),
                  col(d), col(1), pl.BlockSpec((1, TN), lambda i, j: (0, j))],
        out_specs=[pl.BlockSpec((tm, TN), lambda i, j: (i, j)),
                   pl.BlockSpec((tm, LANES), lambda i, j: (i, 0)),
                   pl.BlockSpec((tm, d), lambda i, j: (i, 0))],
        out_shape=[jax.ShapeDtypeStruct((t, Z_TILES * TN), BF16),
                   jax.ShapeDtypeStruct((t, LANES), F32),
                   jax.ShapeDtypeStruct((t, d), BF16)],
        compiler_params=_cparams(("arbitrary", "arbitrary")),
        name="in_projection",
    )(x2d, mod_l, w_in_p, b_in_p, col_scale)


PROJ_PARTS = 2


def _dil_proj_kernel(h_ref, w_ref, b_ref, cs_ref, *refs, tm, dil, rope):
    if rope:
        rc_ref, rp_ref, rm_ref, o_ref, de_scr = refs
    else:
        o_ref, de_scr = refs
    n_part = PROJ_PARTS
    part_rows = tm // n_part
    n_slab = TN // LANES
    accs = []
    for part in range(n_part):
        rows = slice(part * part_rows, (part + 1) * part_rows)
        accs.append((jnp.dot(h_ref[rows, :], w_ref[0], preferred_element_type=F32) + b_ref[0])
                    * cs_ref[...])
    for part, acc in enumerate(accs):
        rows = slice(part * part_rows, (part + 1) * part_rows)
        slabs = [acc[:, s * LANES:(s + 1) * LANES] for s in range(n_slab)]
        if rope:
            rc, rp, rm = rc_ref[rows, :], rp_ref[rows, :], rm_ref[rows, :]
            slabs = [_rope_slab(x, rc, rp, rm, PARTIAL_ROT // 2) for x in slabs]
        if dil == 1:
            o_ref[0, 0, rows, :] = jnp.concatenate(slabs, axis=1).astype(BF16)
        else:
            n = part_rows // dil
            for s in range(n_slab):
                de_scr[part * n_slab + s] = slabs[s]
            for r in range(dil):
                o_ref[0, r, part * n:(part + 1) * n, :] = jnp.concatenate(
                    [de_scr[part * n_slab + s, pl.ds(r, n, stride=dil), :] for s in range(n_slab)],
                    axis=1).astype(BF16)


def dilated_projection(h, w_in_p, b_in_p, col_scale, tabs, g, layer, batch, seq, first_part, n_parts):
    t, d = h.shape
    dil = DILATION_GROUPS[g][1]
    tm = min(1024, seq)
    tiles_per_batch = seq // tm
    tile = lambda p: TILE_B0 + (first_part + p) * N_DIL + g
    col = lambda rows: pl.BlockSpec((1, rows, TN), lambda i, p: (layer, 0, tile(p)))
    in_specs = [pl.BlockSpec((tm, d), lambda i, p: (i, 0)), col(d), col(1),
                pl.BlockSpec((1, TN), lambda i, p: (0, tile(p)))]
    args = [h, w_in_p, b_in_p, col_scale]
    if tabs is not None:
        in_specs += [pl.BlockSpec((tm, LANES), lambda i, p: (i, 0))] * 3
        args += list(tabs)
    return pl.pallas_call(
        functools.partial(_dil_proj_kernel, tm=tm, dil=dil, rope=tabs is not None),
        grid=(t // tm, n_parts),
        in_specs=in_specs,
        out_specs=pl.BlockSpec((1, dil, tm // dil, TN),
                               lambda i, p: (i // tiles_per_batch, 0, i % tiles_per_batch, p)),
        out_shape=jax.ShapeDtypeStruct((batch, dil, seq // dil, n_parts * TN), BF16),
        scratch_shapes=[pltpu.VMEM((PROJ_PARTS * TN // LANES, tm // PROJ_PARTS, LANES), F32)],
        compiler_params=_cparams(("arbitrary", "arbitrary")),
        name="dilated_projection",
    )(*args)


def _in_segments(d):
    o_af = 3 * W_A
    o_b = o_af + A_HEADS
    o_c = o_b + 3 * N_DIL * W_B
    o_gl = o_c + Q_LORA + KV_LORA + D_ROPE
    c_width = o_gl - o_c
    return ((o_gl, 3 * d, 0), (0, o_af, TILE_AQ * TN), (o_c, c_width, TILE_CQ * TN),
            (o_af, A_HEADS, TILE_CKV * TN + AF_LANE0), (o_b, o_c - o_b, TILE_B0 * TN))


def _pack_kernel(w_ref, o_ref, *, segments):
    o_ref[...] = jnp.zeros_like(o_ref)
    for src, width, dst in segments:
        o_ref[0, :, dst:dst + width] = w_ref[0, :, src:src + width].astype(o_ref.dtype)


def pack_in_weights(w_in, out_dtype, rows):
    depth, d_rows, n_in = w_in.shape
    segments = _in_segments((n_in - (3 * W_A + A_HEADS + 3 * N_DIL * W_B + Q_LORA + KV_LORA + D_ROPE))
                            // 3)
    return pl.pallas_call(
        functools.partial(_pack_kernel, segments=segments),
        grid=(depth, d_rows // rows),
        in_specs=[pl.BlockSpec((1, rows, n_in), lambda l, i: (l, i, 0))],
        out_specs=pl.BlockSpec((1, rows, N_TILES * TN), lambda l, i: (l, i, 0)),
        out_shape=jax.ShapeDtypeStruct((depth, d_rows, N_TILES * TN), out_dtype),
        compiler_params=_cparams(("arbitrary", "arbitrary")),
        name="pack_in_weights",
    )(w_in)


def _logf_kernel(zf_ref, lf_ref, carry):
    @pl.when(pl.program_id(1) == 0)
    def _():
        carry[...] = jnp.zeros_like(carry)

    x = zf_ref[...]
    ls = jnp.minimum(x, 0.0) - jnp.log1p(jnp.exp(-jnp.abs(x)))
    ts = x.shape[0]
    row = lax.broadcasted_iota(jnp.int32, (ts, ts), 0)
    col = lax.broadcasted_iota(jnp.int32, (ts, ts), 1)
    tri = (row >= col).astype(F32)
    cs = jnp.dot(tri, ls, precision=HIGHEST, preferred_element_type=F32) + carry[0:1, :]
    lf_ref[...] = cs * LOG2E
    carry[...] = jnp.broadcast_to(cs[ts - 1:ts, :], carry.shape)


def log_forget_cumsum(zf, batch, seq):
    ts = min(256, seq)
    nt = seq // ts
    return pl.pallas_call(
        _logf_kernel,
        grid=(batch, nt),
        in_specs=[pl.BlockSpec((ts, LANES), lambda b, i: (b * nt + i, 0))],
        out_specs=pl.BlockSpec((ts, LANES), lambda b, i: (b * nt + i, 0)),
        out_shape=jax.ShapeDtypeStruct(zf.shape, F32),
        scratch_shapes=[pltpu.VMEM((SUBLANES, LANES), F32)],
        compiler_params=_cparams(("arbitrary", "arbitrary")),
        name="log_forget_cumsum",
    )(zf)


FLASH_TQ = 4096
FLASH_SUB = 512
V_PAD = 16


def _split3(x):
    hi = x.astype(BF16)
    r1 = x - hi.astype(F32)
    mid = r1.astype(BF16)
    lo = (r1 - mid.astype(F32)).astype(BF16)
    return hi, mid, lo


def _flash_kernel(*refs, decay, dv, seq):
    if decay:
        q_ref, k_ref, v_ref, lfq_ref, lfk_ref, o_ref, kext, vext, m_scr, acc_scr = refs
    else:
        q_ref, k_ref, v_ref, o_ref, vext, m_scr, acc_scr = refs
    qi = pl.program_id(2)
    tq = q_ref.shape[0]
    sub = min(FLASH_SUB, tq)
    n_sub = tq // sub
    dn = (((1,), (1,)), ((), ()))

    @pl.when(qi == 0)
    def _():
        lane = lax.broadcasted_iota(jnp.int32, (seq, LANES), 1)
        for c in range(seq // sub):
            cols = slice(c * sub, (c + 1) * sub)
            vext[0:dv, cols] = v_ref[cols, :].astype(F32).T.astype(BF16)
        ones_row = lax.broadcasted_iota(jnp.int32, (V_PAD, seq), 0) == 0
        vext[dv:, :] = jnp.where(ones_row, 1.0, 0.0).astype(BF16)
        if decay:
            hi, mid, lo = _split3(lfk_ref[0, 0])
            kext[:, :HEAD_DIM] = k_ref[...]
            kext[:, HEAD_DIM:] = jnp.where(
                lane == 0, -hi, jnp.where(lane == 1, -mid, jnp.where(
                    lane == 2, -lo, jnp.where(lane < 6, 1.0, 0.0).astype(BF16))))

    if decay:
        lane = lax.broadcasted_iota(jnp.int32, (tq, LANES), 1)
        hi, mid, lo = _split3(lfq_ref[0, 0])
        q_tail = jnp.where(lane == 3, hi, jnp.where(lane == 4, mid, jnp.where(
            lane == 5, lo, jnp.where(lane < 3, 1.0, 0.0).astype(BF16))))
        q_all = jnp.concatenate([q_ref[...], q_tail], axis=1)
        keys = kext
    else:
        q_all = q_ref[...]
        keys = k_ref

    m_scr[...] = jnp.full_like(m_scr, -jnp.inf)
    acc_scr[...] = jnp.zeros_like(acc_scr)

    def update(work):
        cols = [slice(r * sub, (r + 1) * sub) for r, _, _ in work]
        starts = [pl.multiple_of(c * sub, sub) for _, c, _ in work]
        scores = []
        for (r, c, masked), cl, st in zip(work, cols, starts):
            s = lax.dot_general(keys[pl.ds(st, sub), :], q_all[cl], dn, preferred_element_type=F32)
            if masked:
                key = lax.broadcasted_iota(jnp.int32, s.shape, 0)
                qry = lax.broadcasted_iota(jnp.int32, s.shape, 1)
                s = jnp.where(key <= qry, s, -jnp.inf)
            scores.append(s)
        m_prev = [m_scr[0:1, cl] for cl in cols]
        m_new = [jnp.maximum(mp, jnp.max(s, axis=0, keepdims=True)) for mp, s in zip(m_prev, scores)]
        probs = [jnp.exp2(s - mn).astype(BF16) for s, mn in zip(scores, m_new)]
        for cl, st, mp, mn, p in zip(cols, starts, m_prev, m_new, probs):
            acc_scr[:, cl] = (jnp.exp2(mp - mn) * acc_scr[:, cl]
                              + jnp.dot(vext[:, pl.ds(st, sub)], p, preferred_element_type=F32))
            m_scr[:, cl] = jnp.broadcast_to(mn, (SUBLANES, sub))

    def full_chunks(c, carry):
        update([(r, c, False) for r in range(n_sub)])
        return carry

    lax.fori_loop(0, qi * n_sub, full_chunks, 0)
    for d in range(n_sub):
        update([(r, qi * n_sub + d, r == d) for r in range(d, n_sub)])
    for r in range(n_sub):
        cols = slice(r * sub, (r + 1) * sub)
        o_t = acc_scr[0:dv, cols] / acc_scr[dv:dv + 1, cols]
        o_ref[cols, :] = o_t.T.astype(o_ref.dtype)


def flash_attention(q_arr, k_arr, v_arr, q_col0, k_col0, v_col0, dq, dv, heads, batch, seq,
                    decay=None):
    tq = min(FLASH_TQ, seq)
    nq = seq // tq
    in_specs = [
        pl.BlockSpec((tq, dq), lambda b, h, i: (b * nq + i, q_col0 + h)),
        pl.BlockSpec((seq, dq), lambda b, h, i: (b, k_col0 + h)),
        pl.BlockSpec((seq, dv), lambda b, h, i: (b, v_col0 + h)),
    ]
    args = [q_arr, k_arr, v_arr]
    scratch = [pltpu.VMEM((dv + V_PAD, seq), BF16), pltpu.VMEM((SUBLANES, tq), F32),
               pltpu.VMEM((dv + V_PAD, tq), F32)]
    if decay is not None:
        in_specs += [pl.BlockSpec((1, 1, tq, 1), lambda b, h, i: (b, h, i, 0)),
                     pl.BlockSpec((1, 1, seq, 1), lambda b, h, i: (b, h, 0, 0))]
        args += [decay, decay]
        scratch = [pltpu.VMEM((seq, dq + LANES), BF16)] + scratch
    return pl.pallas_call(
        functools.partial(_flash_kernel, decay=decay is not None, dv=dv, seq=seq),
        grid=(batch, heads, nq),
        in_specs=in_specs,
        out_specs=pl.BlockSpec((tq, dv), lambda b, h, i: (b * nq + i, h)),
        out_shape=jax.ShapeDtypeStruct((batch * seq, heads * dv), BF16),
        scratch_shapes=scratch,
        compiler_params=_cparams(("arbitrary", "arbitrary", "arbitrary")),
        name="flash_fox" if decay is not None else "flash_mla",
    )(*args)


DIL_ROWS = 1024


def _dilated_kernel(q_ref, k_ref, v_ref, o_ref, lse_ref, *, rows):
    c = pl.program_id(2)
    wb = WIN_BLOCK
    row = lax.broadcasted_iota(jnp.int32, (wb, wb), 0)
    col = lax.broadcasted_iota(jnp.int32, (wb, wb), 1)
    cur_ok = col <= row
    dn = (((1,), (1,)), ((), ()))
    chains = []
    for n in range(rows // wb):
        base = c * rows + n * wb
        cur = pl.ds(pl.multiple_of(base, wb), wb)
        prev = pl.ds(pl.multiple_of(jnp.maximum(base - wb, 0), wb), wb)
        prev_ok = (col >= row) & (base > 0)
        for h in range(B_HEADS):
            chains.append((slice(n * wb, (n + 1) * wb), slice(h * HEAD_DIM, (h + 1) * HEAD_DIM),
                           cur, prev, prev_ok))
    scores = []
    for qrows, sl, cur, prev, prev_ok in chains:
        q = q_ref[0, 0, qrows, sl]
        s_c = lax.dot_general(q, k_ref[0, 0, cur, sl], dn, preferred_element_type=F32)
        s_p = lax.dot_general(q, k_ref[0, 0, prev, sl], dn, preferred_element_type=F32)
        scores.append((jnp.where(cur_ok, s_c, -jnp.inf), jnp.where(prev_ok, s_p, -jnp.inf)))
    maxes = [jnp.maximum(jnp.max(s_c, axis=1, keepdims=True), jnp.max(s_p, axis=1, keepdims=True))
             for s_c, s_p in scores]
    probs = [(jnp.exp(s_c - m), jnp.exp(s_p - m)) for (s_c, s_p), m in zip(scores, maxes)]
    sums = [jnp.sum(p_c, axis=1, keepdims=True) + jnp.sum(p_p, axis=1, keepdims=True)
            for p_c, p_p in probs]
    for (qrows, sl, cur, prev, _), (p_c, p_p), m, l in zip(chains, probs, maxes, sums):
        o = (jnp.dot(p_c.astype(BF16), v_ref[0, 0, cur, sl], preferred_element_type=F32)
             + jnp.dot(p_p.astype(BF16), v_ref[0, 0, prev, sl], preferred_element_type=F32))
        o_ref[0, 0, qrows, sl] = (o / l).astype(BF16)
        lse_ref[0, 0, qrows, sl] = jnp.broadcast_to(m + jnp.log(l), (wb, HEAD_DIM))


def dilated_attention(zqk, zv, batch):
    _, dil, l_sub, _ = zv.shape
    rows = min(DIL_ROWS, l_sub)
    chunk = lambda c: pl.BlockSpec((1, 1, rows, W_B), lambda b, r, i: (b, r, i, c))
    whole = lambda c: pl.BlockSpec((1, 1, l_sub, W_B), lambda b, r, i: (b, r, 0, c))
    return pl.pallas_call(
        functools.partial(_dilated_kernel, rows=rows),
        grid=(batch, dil, l_sub // rows),
        in_specs=[chunk(0), whole(1), whole(0)],
        out_specs=[chunk(0), chunk(0)],
        out_shape=[jax.ShapeDtypeStruct((batch, dil, l_sub, W_B), BF16),
                   jax.ShapeDtypeStruct((batch, dil, l_sub, W_B), F32)],
        compiler_params=_cparams(("arbitrary", "arbitrary", "arbitrary")),
        name="dilated_attention",
    )(zqk, zqk, zv)


def _rms(x, g):
    return x * lax.rsqrt(jnp.mean(x * x, axis=-1, keepdims=True) + LN_EPS) * g


def _mla_up_kernel(cq_ref, ckv_ref, gq_ref, gkv_ref, wq_ref, wk_ref, wv_ref, rc_ref, rp_ref, rm_ref,
                   q_ref, k_ref, v_ref):
    scale = (D_NOPE + D_ROPE) ** -0.5 * LOG2E
    half = D_ROPE // 2
    rc, rp, rm = rc_ref[...], rp_ref[...], rm_ref[...]
    cqn = _rms(cq_ref[...].astype(F32), gq_ref[...]).astype(BF16)
    q = jnp.dot(cqn, wq_ref[...], preferred_element_type=F32) * scale
    ckvr = ckv_ref[...].astype(F32)
    ckvn = _rms(ckvr[:, :KV_LORA], gkv_ref[...]).astype(BF16)
    k_rope = _rope_slab(ckvr[:, KV_LORA:KV_LORA + LANES], rc, rp, rm, half).astype(BF16)
    k_nope = jnp.dot(ckvn, wk_ref[...], preferred_element_type=F32)
    v_ref[...] = jnp.dot(ckvn, wv_ref[...], preferred_element_type=F32).astype(BF16)
    for h in range(C_HEADS):
        lo = h * 2 * LANES
        q_ref[:, lo:lo + LANES] = q[:, lo:lo + LANES].astype(BF16)
        q_ref[:, lo + LANES:lo + 2 * LANES] = _rope_slab(
            q[:, lo + LANES:lo + 2 * LANES], rc, rp, rm, half).astype(BF16)
        k_ref[:, lo:lo + LANES] = k_nope[:, h * LANES:(h + 1) * LANES].astype(BF16)
        k_ref[:, lo + LANES:lo + 2 * LANES] = k_rope


def mla_up_projection(z, g_cq, g_ckv, wq_p, wk_p, wv_p, tabs_c):
    t = z.shape[0]
    tm = min(512, t)
    rc, rp, rm = tabs_c
    full = lambda a: pl.BlockSpec(a.shape, lambda i: (0,) * a.ndim)
    tab = pl.BlockSpec((tm, LANES), lambda i: (i, 0))
    row = lambda w: pl.BlockSpec((tm, w), lambda i: (i, 0))
    return pl.pallas_call(
        _mla_up_kernel,
        grid=(t // tm,),
        in_specs=[pl.BlockSpec((tm, TN), lambda i: (i, TILE_CQ)),
                  pl.BlockSpec((tm, TN), lambda i: (i, TILE_CKV)),
                  full(g_cq), full(g_ckv), full(wq_p), full(wk_p), full(wv_p), tab, tab, tab],
        out_specs=[row(C_HEADS * 2 * LANES), row(C_HEADS * 2 * LANES), row(W_C)],
        out_shape=[jax.ShapeDtypeStruct((t, C_HEADS * 2 * LANES), BF16),
                   jax.ShapeDtypeStruct((t, C_HEADS * 2 * LANES), BF16),
                   jax.ShapeDtypeStruct((t, W_C), BF16)],
        compiler_params=_cparams(("arbitrary",)),
        name="mla_up_projection",
    )(z, z, g_cq, g_ckv, wq_p, wk_p, wv_p, rc, rp, rm)


def pack_mla_weights(w_uq_l, w_ukv_l):
    wq = w_uq_l.reshape(Q_LORA, C_HEADS, D_NOPE + D_ROPE)
    wq = jnp.pad(wq, ((0, 0), (0, 0), (0, 2 * LANES - D_NOPE - D_ROPE)))
    wkv = w_ukv_l.reshape(KV_LORA, C_HEADS, D_NOPE + D_VC)
    wk = wkv[:, :, :D_NOPE].reshape(KV_LORA, C_HEADS * D_NOPE)
    wv = wkv[:, :, D_NOPE:].reshape(KV_LORA, W_C)
    return (wq.reshape(Q_LORA, C_HEADS * 2 * LANES).astype(BF16), wk.astype(BF16), wv.astype(BF16))


def _merge_kernel(gla_ref, glb_ref, glc_ref, oa_ref, ob0_ref, ob1_ref, ob2_ref,
                  ls0_ref, ls1_ref, ls2_ref, oc_ref, wa_ref, wb_ref, wc_ref, y_ref,
                  o_scr, l_scr, *, tm):
    for g, (o_ref, ls_ref) in enumerate(((ob0_ref, ls0_ref), (ob1_ref, ls1_ref), (ob2_ref, ls2_ref))):
        dil = DILATION_GROUPS[g][1]
        for h in range(B_HEADS):
            sl = slice(h * HEAD_DIM, (h + 1) * HEAD_DIM)
            for r in range(dil):
                rows = pl.ds(r, tm // dil, stride=dil) if dil > 1 else slice(None)
                o_scr[g * B_HEADS + h, rows, :] = o_ref[0, r, :, sl].astype(F32)
                l_scr[g * B_HEADS + h, rows, :] = ls_ref[0, r, :, sl]
    slabs = []
    for h in range(B_HEADS):
        l0, l1, l2 = l_scr[h], l_scr[B_HEADS + h], l_scr[2 * B_HEADS + h]
        m = jnp.maximum(jnp.maximum(l0, l1), l2)
        e0, e1, e2 = jnp.exp(l0 - m), jnp.exp(l1 - m), jnp.exp(l2 - m)
        den = e0 + e1 + e2
        slabs.append((e0 / den) * o_scr[h] + (e1 / den) * o_scr[B_HEADS + h]
                     + (e2 / den) * o_scr[2 * B_HEADS + h])
    o_b = jnp.concatenate(slabs, axis=1).astype(BF16)
    pa = jnp.dot(oa_ref[...], wa_ref[...], preferred_element_type=F32)
    pb = jnp.dot(o_b, wb_ref[...], preferred_element_type=F32)
    pc = jnp.dot(oc_ref[...], wc_ref[...], preferred_element_type=F32)
    y = (jax.nn.sigmoid(gla_ref[...].astype(F32)) * pa
         + jax.nn.sigmoid(glb_ref[...].astype(F32)) * pb
         + jax.nn.sigmoid(glc_ref[...].astype(F32)) * pc)
    y_ref[...] = y.astype(BF16)


def branch_merge(z, o_a, obs, lses, o_c, wa, wb, wc, batch, seq):
    t = z.shape[0]
    d = wa.shape[1]
    tm = min(512, seq)
    tpb = seq // tm
    full = lambda a: pl.BlockSpec(a.shape, lambda i: (0,) * a.ndim)
    gl = lambda c: pl.BlockSpec((tm, d), lambda i: (i, c))
    res = lambda dil: pl.BlockSpec((1, dil, tm // dil, W_B), lambda i: (i // tpb, 0, i % tpb, 0))
    res_specs = [res(dil) for _, dil in DILATION_GROUPS]
    return pl.pallas_call(
        functools.partial(_merge_kernel, tm=tm),
        grid=(t // tm,),
        in_specs=[gl(0), gl(1), gl(2), pl.BlockSpec((tm, W_A), lambda i: (i, 0))]
        + res_specs + res_specs
        + [pl.BlockSpec((tm, W_C), lambda i: (i, 0)), full(wa), full(wb), full(wc)],
        out_specs=pl.BlockSpec((tm, d), lambda i: (i, 0)),
        out_shape=jax.ShapeDtypeStruct((t, d), BF16),
        scratch_shapes=[pltpu.VMEM((N_DIL * B_HEADS, tm, HEAD_DIM), F32),
                        pltpu.VMEM((N_DIL * B_HEADS, tm, HEAD_DIM), F32)],
        compiler_params=_cparams(("arbitrary",)),
        name="branch_merge",
    )(z, z, z, o_a, *obs, *lses, o_c, wa, wb, wc)


def _layer_norm(x, g, b):
    mu = jnp.mean(x, axis=-1, keepdims=True)
    xc = x - mu
    var = jnp.mean(xc * xc, axis=-1, keepdims=True)
    return xc * lax.rsqrt(var + LN_EPS) * g + b


def _out_router_kernel(y_ref, x_ref, mod_ref, wo_ref, lng_ref, lnb_ref, wr_ref, br_ref,
                       x1_ref, h2_ref, ri_ref, rf_ref, cnt_ref, carry, *, tm, alpha):
    @pl.when(pl.program_id(0) == 0)
    def _():
        carry[...] = jnp.zeros_like(carry)

    gate1 = mod_ref[0, 2:3, :]
    shift2 = mod_ref[0, 3:4, :]
    scale2 = mod_ref[0, 4:5, :]
    mix = jnp.dot(y_ref[...], wo_ref[...], preferred_element_type=F32)
    x1 = _layer_norm(alpha * x_ref[...] + (1.0 + gate1) * mix, lng_ref[...], lnb_ref[...])
    x1_ref[...] = x1
    h2 = x1 * (1.0 + scale2) + shift2
    d = h2.shape[1]
    for c in range(d // LANES):
        h2_ref[pl.ds(c, tm, stride=d // LANES), :] = h2[:, c * LANES:(c + 1) * LANES]

    h_hi = h2.astype(BF16)
    h_lo = (h2 - h_hi.astype(F32)).astype(BF16)
    logits = (jnp.dot(h_hi, wr_ref[0], preferred_element_type=F32)
              + jnp.dot(h_lo, wr_ref[0], preferred_element_type=F32)
              + jnp.dot(h_hi, wr_ref[1], preferred_element_type=F32))
    s = jax.nn.sigmoid(logits.T[:N_EXPERTS])
    sel = s + br_ref[...]
    sel_m = [sel[i * N_GROUPS:(i + 1) * N_GROUPS] for i in range(EXPERTS_PER_GROUP)]
    s_m = [s[i * N_GROUPS:(i + 1) * N_GROUPS] for i in range(EXPERTS_PER_GROUP)]
    gscore = None
    for a in range(EXPERTS_PER_GROUP):
        for b in range(a + 1, EXPERTS_PER_GROUP):
            pair = sel_m[a] + sel_m[b]
            gscore = pair if gscore is None else jnp.maximum(gscore, pair)
    grp = lax.broadcasted_iota(jnp.int32, (N_GROUPS, tm), 0).astype(F32)
    gmax = jnp.max(gscore, axis=0, keepdims=True)
    gbest = jnp.min(jnp.where(gscore == gmax, grp, float(N_GROUPS)), axis=0, keepdims=True)
    in_best = grp == gbest
    v = [jnp.sum(jnp.where(in_best, m, 0.0), axis=0, keepdims=True) for m in sel_m]
    u = [jnp.sum(jnp.where(in_best, m, 0.0), axis=0, keepdims=True) for m in s_m]

    def first_argmax(vals):
        best = vals[0]
        for x in vals[1:]:
            best = jnp.maximum(best, x)
        idx = jnp.full_like(gbest, float(EXPERTS_PER_GROUP - 1))
        for i in range(EXPERTS_PER_GROUP - 2, -1, -1):
            idx = jnp.where(vals[i] == best, float(i), idx)
        return idx

    i1 = first_argmax(v)
    i2 = first_argmax([jnp.where(i1 == i, -jnp.inf, v[i]) for i in range(EXPERTS_PER_GROUP)])

    def member(vals, idx):
        out = vals[EXPERTS_PER_GROUP - 1]
        for i in range(EXPERTS_PER_GROUP - 2, -1, -1):
            out = jnp.where(idx == i, vals[i], out)
        return out

    u1, u2 = member(u, i1), member(u, i2)
    w1 = u1 / (u1 + u2)
    w2 = u2 / (u1 + u2)
    e1 = gbest * EXPERTS_PER_GROUP + i1
    e2 = gbest * EXPERTS_PER_GROUP + i2
    row1 = i1 * N_GROUPS + gbest
    row2 = i2 * N_GROUPS + gbest

    erow = lax.broadcasted_iota(jnp.int32, (N_EXPERTS, tm), 0).astype(F32)
    hit1, hit2 = erow == row1, erow == row2
    onehot = (hit1 | hit2).astype(F32)
    t_row = lax.broadcasted_iota(jnp.int32, (tm, tm), 0)
    t_col = lax.broadcasted_iota(jnp.int32, (tm, tm), 1)
    earlier = (t_row < t_col).astype(BF16)
    cnt = jnp.dot(onehot.astype(BF16), earlier, preferred_element_type=F32) + carry[:, 0:1]
    r1 = jnp.sum(jnp.where(hit1, cnt, 0.0), axis=0, keepdims=True)
    r2 = jnp.sum(jnp.where(hit2, cnt, 0.0), axis=0, keepdims=True)
    new_carry = carry[...] + jnp.sum(onehot, axis=1, keepdims=True)
    carry[...] = new_carry
    cnt_ref[...] = new_carry

    sub = lax.broadcasted_iota(jnp.int32, (SUBLANES, tm), 0)
    ri = jnp.where(sub == 0, e1, jnp.where(sub == 1, e2, jnp.where(sub == 2, r1, jnp.where(
        sub == 3, r2, 0.0))))
    ri_ref[0] = ri.astype(jnp.int32)
    rf_ref[0] = jnp.where(sub == 0, w1, jnp.where(sub == 1, w2, 0.0))


def out_proj_router(y, x2d, mod_l, w_o, ln_g, ln_b, wr_p, br_p, batch, seq, alpha):
    t, d = x2d.shape
    tm = min(256, seq)
    tpb = seq // tm
    nt = t // tm
    full = lambda a: pl.BlockSpec(a.shape, lambda i: (0,) * a.ndim)
    tile = pl.BlockSpec((tm, d), lambda i: (i, 0))
    rows = pl.BlockSpec((1, SUBLANES, tm), lambda i: (i, 0, 0))
    rt = d // LANES
    return pl.pallas_call(
        functools.partial(_out_router_kernel, tm=tm, alpha=alpha),
        grid=(nt,),
        in_specs=[tile, tile, pl.BlockSpec((1, 6, d), lambda i: (i // tpb, 0, 0)),
                  full(w_o), full(ln_g), full(ln_b), full(wr_p), full(br_p)],
        out_specs=[tile, pl.BlockSpec((tm * rt, LANES), lambda i: (i, 0)), rows, rows,
                   pl.BlockSpec((N_EXPERTS, LANES), lambda i: (0, 0))],
        out_shape=[jax.ShapeDtypeStruct((t, d), F32),
                   jax.ShapeDtypeStruct((t * rt, LANES), F32),
                   jax.ShapeDtypeStruct((nt, SUBLANES, tm), jnp.int32),
                   jax.ShapeDtypeStruct((nt, SUBLANES, tm), F32),
                   jax.ShapeDtypeStruct((N_EXPERTS, LANES), F32)],
        scratch_shapes=[pltpu.VMEM((N_EXPERTS, LANES), F32)],
        compiler_params=_cparams(("arbitrary",)),
        name="out_proj_router",
    )(y, x2d, mod_l, w_o, ln_g, ln_b, wr_p, br_p)


def _dispatch_kernel(dest_ref, h_ref, xs_in_ref, xs_ref, sem, *, tm, rt):
    del xs_in_ref

    def row_copy(a, t):
        return pltpu.make_async_copy(h_ref.at[pl.ds(t * rt, rt), :], xs_ref.at[dest_ref[0, 0, a]], sem)

    def issue(t, carry):
        row_copy(t, t).start()
        row_copy(tm + t, t).start()
        return carry

    def drain(t, carry):
        row_copy(t, t).wait()
        row_copy(tm + t, t).wait()
        return carry

    lax.fori_loop(0, tm, issue, 0, unroll=8)
    lax.fori_loop(0, tm, drain, 0, unroll=8)


def moe_dispatch(dest_tiles, h2t, xs_buf, tm):
    nt = dest_tiles.shape[0]
    rt = h2t.shape[0] // (nt * tm)
    return pl.pallas_call(
        functools.partial(_dispatch_kernel, tm=tm, rt=rt),
        grid=(nt,),
        in_specs=[pl.BlockSpec((1, 1, 2 * tm), lambda i: (i, 0, 0), memory_space=pltpu.SMEM),
                  pl.BlockSpec((tm * rt, LANES), lambda i: (i, 0)),
                  pl.BlockSpec(memory_space=pl.ANY)],
        out_specs=pl.BlockSpec(memory_space=pl.ANY),
        out_shape=jax.ShapeDtypeStruct(xs_buf.shape, xs_buf.dtype),
        scratch_shapes=[pltpu.SemaphoreType.DMA(())],
        input_output_aliases={2: 0},
        compiler_params=_cparams(("arbitrary",)),
        name="moe_dispatch",
    )(dest_tiles, h2t, xs_buf)


def _expert_kernel(be_ref, nu_ref, xs_ref, wg_ref, wu_ref, wd_ref, ys_ref, wg_s, wu_s, wd_s, *, rt):
    n = pl.program_id(0)

    @pl.when(n < nu_ref[0])
    def _():
        prev = be_ref[jnp.maximum(n - 1, 0)]

        @pl.when((n == 0) | (be_ref[n] != prev))
        def _():
            wg_s[...] = wg_ref[0, 0].astype(BF16)
            wu_s[...] = wu_ref[0, 0].astype(BF16)
            wd_s[...] = wd_ref[0, 0].astype(BF16)

        pr = MOE_BLOCK // MOE_PARTS

        def ffn(parts):
            xs = [jnp.concatenate([xs_ref[pl.ds(p * pr * rt + c, pr, stride=rt), :]
                                   for c in range(rt)], axis=1).astype(BF16) for p in parts]
            gates = [jnp.dot(x, wg_s[...], preferred_element_type=F32) for x in xs]
            ups = [jnp.dot(x, wu_s[...], preferred_element_type=F32) for x in xs]
            acts = [(g * jax.nn.sigmoid(g) * u).astype(BF16) for g, u in zip(gates, ups)]
            ys = [jnp.dot(a, wd_s[...], preferred_element_type=F32) for a in acts]
            for p, y in zip(parts, ys):
                for c in range(rt):
                    ys_ref[pl.ds(p * pr * rt + c, pr, stride=rt), :] = y[:, c * LANES:(c + 1) * LANES]

        lead = (MOE_PARTS - 1) * pr

        @pl.when(nu_ref[1 + n] > lead)
        def _():
            ffn(list(range(MOE_PARTS)))

        @pl.when(nu_ref[1 + n] <= lead)
        def _():
            ffn(list(range(MOE_PARTS - 1)))
            ys_ref[lead * rt:, :] = jnp.zeros((pr * rt, LANES), F32)

    @pl.when(n >= nu_ref[0])
    def _():
        ys_ref[...] = jnp.zeros_like(ys_ref)


def expert_ffn(blk_e, n_used, xs2d, w_gate, w_up, w_down, layer):
    depth, n_exp, d, de = w_gate.shape
    rt = d // LANES
    n_blk = xs2d.shape[0] // (MOE_BLOCK * rt)
    rows = lambda n, be, nu: (jnp.minimum(n, nu[0] - 1), 0)
    grid_spec = pltpu.PrefetchScalarGridSpec(
        num_scalar_prefetch=2,
        grid=(n_blk,),
        in_specs=[pl.BlockSpec((MOE_BLOCK * rt, LANES), rows),
                  pl.BlockSpec((1, 1, d, de), lambda n, be, nu: (layer, be[n], 0, 0)),
                  pl.BlockSpec((1, 1, d, de), lambda n, be, nu: (layer, be[n], 0, 0)),
                  pl.BlockSpec((1, 1, de, d), lambda n, be, nu: (layer, be[n], 0, 0))],
        out_specs=pl.BlockSpec((MOE_BLOCK * rt, LANES), lambda n, be, nu: (n, 0)),
        scratch_shapes=[pltpu.VMEM((d, de), BF16), pltpu.VMEM((d, de), BF16),
                        pltpu.VMEM((de, d), BF16)])
    return pl.pallas_call(
        functools.partial(_expert_kernel, rt=rt),
        grid_spec=grid_spec,
        out_shape=jax.ShapeDtypeStruct(xs2d.shape, F32),
        compiler_params=_cparams(("arbitrary",)),
        name="expert_ffn",
    )(blk_e, n_used, xs2d, w_gate, w_up, w_down)


def _combine_kernel(dest_ref, dnext_ref, ys_ref, rf_ref, x_ref, mod_ref, lng_ref, lnb_ref, o_ref,
                    buf, sem, *, tm, rt, alpha, nt):
    i = pl.program_id(0)
    slot = i % 2

    def row_copy(dref, a, sl):
        return pltpu.make_async_copy(ys_ref.at[dref[0, 0, a]], buf.at[sl, pl.ds(a * rt, rt), :],
                                     sem.at[sl])

    def issue(dref, sl):
        def body(a, carry):
            row_copy(dref, a, sl).start()
            return carry
        lax.fori_loop(0, 2 * tm, body, 0, unroll=8)

    @pl.when(i == 0)
    def _():
        issue(dest_ref, 0)

    @pl.when(i + 1 < nt)
    def _():
        issue(dnext_ref, 1 - slot)

    def drain(a, carry):
        row_copy(dest_ref, a, slot).wait()
        return carry

    lax.fori_loop(0, 2 * tm, drain, 0, unroll=8)
    cur = buf.at[slot]

    def rows(k):
        return jnp.concatenate(
            [cur[pl.ds(k * tm * rt + c, tm, stride=rt), :] for c in range(rt)], axis=1)

    ff = rf_ref[:, 0:1] * rows(0) + rf_ref[:, 1:2] * rows(1)
    gate2 = mod_ref[0, 5:6, :]
    o_ref[...] = _layer_norm(alpha * x_ref[...] + (1.0 + gate2) * ff, lng_ref[...], lnb_ref[...])


def moe_combine(dest_tiles, ys3d, rf, x1, mod_l, ln_g, ln_b, batch, seq, tm, alpha):
    t, d = x1.shape
    rt = d // LANES
    tpb = seq // tm
    full = lambda a: pl.BlockSpec(a.shape, lambda i: (0,) * a.ndim)
    tile = pl.BlockSpec((tm, d), lambda i: (i, 0))
    nt = t // tm
    return pl.pallas_call(
        functools.partial(_combine_kernel, tm=tm, rt=rt, alpha=alpha, nt=nt),
        grid=(nt,),
        in_specs=[pl.BlockSpec((1, 1, 2 * tm), lambda i: (i, 0, 0), memory_space=pltpu.SMEM),
                  pl.BlockSpec((1, 1, 2 * tm), lambda i: (jnp.minimum(i + 1, nt - 1), 0, 0),
                               memory_space=pltpu.SMEM),
                  pl.BlockSpec(memory_space=pl.ANY),
                  pl.BlockSpec((tm, TOP_K), lambda i: (i, 0)),
                  tile, pl.BlockSpec((1, 6, d), lambda i: (i // tpb, 0, 0)), full(ln_g), full(ln_b)],
        out_specs=tile,
        out_shape=jax.ShapeDtypeStruct((t, d), F32),
        scratch_shapes=[pltpu.VMEM((2, 2 * tm * rt, LANES), F32), pltpu.SemaphoreType.DMA((2,))],
        compiler_params=_cparams(("arbitrary",)),
        name="moe_combine",
    )(dest_tiles, dest_tiles, ys3d, rf, x1, mod_l, ln_g, ln_b)


def _routing_tables(ri, cnt, n_rows, tm):
    nt = ri.shape[0]
    counts = cnt[:, 0].astype(jnp.int32)
    counts = counts.reshape(EXPERTS_PER_GROUP, N_GROUPS).T.reshape(N_EXPERTS)
    padded = (counts + MOE_BLOCK - 1) // MOE_BLOCK * MOE_BLOCK
    pend = jnp.cumsum(padded)
    pstart = pend - padded
    e_sel = ri[:, 0:TOP_K, :, None] == jnp.arange(N_EXPERTS, dtype=jnp.int32)
    dest = jnp.sum(jnp.where(e_sel, pstart, 0), axis=-1) + ri[:, TOP_K:2 * TOP_K, :]
    dest_tiles = dest.reshape(nt, 1, TOP_K * tm)
    n_blk = n_rows // MOE_BLOCK
    blk_start = jnp.arange(n_blk, dtype=jnp.int32) * MOE_BLOCK
    blk_e = jnp.minimum(jnp.sum(pend[None, :] <= blk_start[:, None], axis=1), N_EXPERTS - 1)
    blk_sel = blk_e[:, None] == jnp.arange(N_EXPERTS, dtype=jnp.int32)
    seg_end = jnp.sum(jnp.where(blk_sel, pstart + counts, 0), axis=1)
    blk_valid = jnp.clip(seg_end - blk_start, 0, MOE_BLOCK)
    n_used = jnp.concatenate([pend[-1:] // MOE_BLOCK, blk_valid]).astype(jnp.int32)
    return dest_tiles.astype(jnp.int32), blk_e.astype(jnp.int32), n_used


def kernel(x, c, positions, w_ada, b_ada, w_in, b_in, g_cq, g_ckv, w_uq, w_ukv, w_pa, w_pb, w_pc, w_o,
           ln1_g, ln1_b, w_router, b_router, w_gate, w_up, w_down, ln2_g, ln2_b):
    batch, seq, d = x.shape
    depth = w_ada.shape[0]
    t = batch * seq
    alpha = (2 * depth) ** 0.25
    rt = d // LANES
    assert rt == ROW_TILE and seq % 2048 == 0 and d % TN == 0
    tm_moe = min(256, seq)

    c_pad = jnp.zeros((SUBLANES, d), F32).at[:batch].set(c)
    mod_all = ada_modulation(c_pad, w_ada, b_ada)[:, :batch].reshape(depth, batch, 6, d)
    pos_col = positions.reshape(t, 1)
    tabs_b = rope_tables(pos_col, PARTIAL_ROT)
    tabs_c = rope_tables(pos_col, D_ROPE)
    qk_scale = HEAD_DIM ** -0.5
    col_scale = jnp.ones((1, N_TILES * TN), F32)
    col_scale = col_scale.at[:, TILE_AQ * TN:(TILE_AQ + 1) * TN].set(qk_scale * LOG2E)
    col_scale = col_scale.at[:, TILE_B0 * TN:(TILE_B0 + N_DIL) * TN].set(qk_scale)

    perm = np.arange(N_EXPERTS).reshape(N_GROUPS, EXPERTS_PER_GROUP).T.reshape(-1)
    wr_f = jnp.zeros((d, LANES), F32).at[:, :N_EXPERTS].set(w_router.astype(F32)[:, perm])
    wr_hi = wr_f.astype(BF16)
    wr_p = jnp.stack([wr_hi, (wr_f - wr_hi.astype(F32)).astype(BF16)])
    br_p = b_router.astype(F32)[perm][:, None]

    n_rows = -(-(t * TOP_K + N_EXPERTS * (MOE_BLOCK - 1)) // MOE_BLOCK) * MOE_BLOCK
    xs_buf = jnp.zeros((n_rows, rt, LANES), F32)

    w_in_p = pack_in_weights(w_in, BF16, rows=128)
    b_in_p = pack_in_weights(b_in[:, None, :], F32, rows=1)

    x2d = x.reshape(t, d)
    for l in range(depth):
        z, zf, h = in_projection(x2d, mod_all[l], w_in_p, b_in_p, col_scale, l, batch, seq)
        zqks = [dilated_projection(h, w_in_p, b_in_p, col_scale, tabs_b, g, l, batch, seq, 0, 2)
                for g in range(N_DIL)]
        zvs = [dilated_projection(h, w_in_p, b_in_p, col_scale, None, g, l, batch, seq, 2, 1)
               for g in range(N_DIL)]

        lf = log_forget_cumsum(zf, batch, seq)[:, :A_HEADS].reshape(batch, seq, A_HEADS)
        lf = lf.transpose(0, 2, 1)[:, :, :, None]
        o_a = flash_attention(z, z, z, TILE_AQ * 4, TILE_AQ * 4 + 4, TILE_AQ * 4 + 8, HEAD_DIM,
                              HEAD_DIM, A_HEADS, batch, seq, decay=lf)

        obs, lses = [], []
        for zqk, zv in zip(zqks, zvs):
            o_g, lse_g = dilated_attention(zqk, zv, batch)
            obs.append(o_g)
            lses.append(lse_g)

        wq_p, wk_p, wv_p = pack_mla_weights(w_uq[l], w_ukv[l])
        q_c, k_c, v_c = mla_up_projection(z, g_cq[l][None, :], g_ckv[l][None, :], wq_p, wk_p, wv_p,
                                          tabs_c)
        o_c = flash_attention(q_c, k_c, v_c, 0, 0, 0, 2 * LANES, D_VC, C_HEADS, batch, seq)

        y = branch_merge(z, o_a, obs, lses, o_c, w_pa[l].astype(BF16), w_pb[l].astype(BF16),
                         w_pc[l].astype(BF16), batch, seq)
        x1, h2t, ri, rf, cnt = out_proj_router(y, x2d, mod_all[l], w_o[l].astype(BF16),
                                               ln1_g[l][None, :], ln1_b[l][None, :], wr_p, br_p,
                                               batch, seq, alpha)

        dest_tiles, blk_e, n_used = _routing_tables(ri, cnt, n_rows, tm_moe)
        wts = rf[:, 0:TOP_K, :].transpose(0, 2, 1).reshape(t, TOP_K)
        xs_buf = moe_dispatch(dest_tiles, h2t, xs_buf, tm_moe)
        ys = expert_ffn(blk_e, n_used, xs_buf.reshape(n_rows * rt, LANES), w_gate, w_up, w_down, l)
        x2d = moe_combine(dest_tiles, ys.reshape(n_rows, rt, LANES), wts, x1, mod_all[l],
                          ln2_g[l][None, :], ln2_b[l][None, :], batch, seq, tm_moe, alpha)
    return x2d.reshape(batch, seq, d)
```

```python
import functools

import jax
import jax.numpy as jnp
import numpy as np
from jax import lax
from jax.experimental import pallas as pl
from jax.experimental.pallas import tpu as pltpu

F32 = jnp.float32
BF16 = jnp.bfloat16
HIGHEST = lax.Precision.HIGHEST

LANES = 128
SUBLANES = 8
VMEM_LIMIT = 56 * 1024 * 1024

HEAD_DIM = 128
A_HEADS = 4
B_HEADS = 4
DILATION_GROUPS = ((128, 1), (512, 4), (2048, 16))
N_DIL = 3
C_HEADS = 6
Q_LORA = 512
KV_LORA = 256
D_NOPE = 128
D_ROPE = 64
D_VC = 128
ROPE_THETA = 500000.0
PARTIAL_ROT = HEAD_DIM // 4
WIN_BLOCK = 128
N_EXPERTS = 32
N_GROUPS = 8
EXPERTS_PER_GROUP = N_EXPERTS // N_GROUPS
TOP_K = 2
LN_EPS = 1e-5
LOG2E = 1.4426950408889634
W_A =A_HEADS * HEAD_DIM
W_B = B_HEADS * HEAD_DIM
W_C = C_HEADS * D_VC

TN = 512
TILE_AQ = 12
TILE_CQ = 15
TILE_CKV = 16
TILE_B0 = 17
N_TILES = 26
Z_TILES = 17
AF_LANE0 = KV_LORA + LANES
ROW_TILE = 16
MOE_BLOCK = 512
MOE_PARTS = 2


def _cparams(sem):
    return pltpu.CompilerParams(dimension_semantics=sem, vmem_limit_bytes=VMEM_LIMIT)


def _ada_kernel(c_ref, w_ref, b_ref, o_ref):
    o_ref[0] = jnp.dot(c_ref[...], w_ref[0], precision=HIGHEST,
                       preferred_element_type=F32) + b_ref[0]


def ada_modulation(c_pad, w_ada, b_ada):
    depth, d, n = w_ada.shape
    tn = 1024
    return pl.pallas_call(
        _ada_kernel,
        grid=(depth, n // tn),
        in_specs=[pl.BlockSpec((SUBLANES, d), lambda l, j: (0, 0)),
                  pl.BlockSpec((1, d, tn), lambda l, j: (l, 0, j)),
                  pl.BlockSpec((1, 1, tn), lambda l, j: (l, 0, j))],
        out_specs=pl.BlockSpec((1, SUBLANES, tn), lambda l, j: (l, 0, j)),
        out_shape=jax.ShapeDtypeStruct((depth, SUBLANES, n), F32),
        compiler_params=_cparams(("arbitrary", "arbitrary")),
        name="ada_modulation",
    )(c_pad, w_ada, b_ada.reshape(depth, 1, n))


def _rope_table_kernel(pos_ref, invf_ref, mhi_ref, mlo_ref, c_ref, sp_ref, sm_ref):
    ang = pos_ref[...].astype(F32) * invf_ref[...]
    sin = jnp.sin(ang)
    c_ref[...] = jnp.cos(ang)
    sp_ref[...] = sin * mhi_ref[...]
    sm_ref[...] = -sin * mlo_ref[...]


def rope_tables(pos_col, rot_dim):
    t = pos_col.shape[0]
    half = rot_dim // 2
    inv_freq = 1.0 / (ROPE_THETA ** (jnp.arange(0, rot_dim, 2, dtype=F32) / rot_dim))
    lane = np.arange(LANES)
    invf = jnp.where(lane < rot_dim, jnp.tile(inv_freq, LANES // half), 0.0).astype(F32)[None, :]
    mhi = jnp.asarray(((lane >= half) & (lane < rot_dim)).astype(np.float32))[None, :]
    mlo = jnp.asarray((lane < half).astype(np.float32))[None, :]
    tt = min(t, 1024)
    row = pl.BlockSpec((1, LANES), lambda i: (0, 0))
    out = pl.BlockSpec((tt, LANES), lambda i: (i, 0))
    return pl.pallas_call(
        _rope_table_kernel,
        grid=(t // tt,),
        in_specs=[pl.BlockSpec((tt, 1), lambda i: (i, 0)), row, row, row],
        out_specs=[out, out, out],
        out_shape=[jax.ShapeDtypeStruct((t, LANES), F32)] * 3,
        compiler_params=_cparams(("arbitrary",)),
        name="rope_tables",
    )(pos_col, invf, mhi, mlo)


def _rope_slab(x, c, sp, sm, half):
    return x * c + pltpu.roll(x, half, 1) * sp + pltpu.roll(x, LANES - half, 1) * sm


def _in_proj_kernel(x_ref, mod_ref, w_ref, b_ref, cs_ref, z_ref, zf_ref, h_ref):
    j = pl.program_id(1)

    @pl.when(j == 0)
    def _():
        shift = mod_ref[0, 0:1, :]
        scale = mod_ref[0, 1:2, :]
        h_ref[...] = (x_ref[...] * (1.0 + scale) + shift).astype(BF16)

    acc = (jnp.dot(h_ref[...], w_ref[0], preferred_element_type=F32) + b_ref[0]) * cs_ref[...]
    z_ref[...] = acc.astype(BF16)

    @pl.when(j == TILE_CKV)
    def _():
        zf_ref[...] = acc[:, AF_LANE0:AF_LANE0 + LANES]


def in_projection(x2d, mod_l, w_in_p, b_in_p, col_scale, layer, batch, seq):
    t, d = x2d.shape
    tm = min(1024, seq)
    tiles_per_batch = seq // tm
    col = lambda rows: pl.BlockSpec((1, rows, TN), lambda i, j: (layer, 0, j))
    return pl.pallas_call(
        _in_proj_kernel,
        grid=(t // tm, Z_TILES),
        in_specs=[pl.BlockSpec((tm, d), lambda i, j: (i, 0)),
                  pl.BlockSpec((1, 6, d), lambda i, j: (i // tiles_per_batch, 0, 0)),
                  col(d), col(1), pl.BlockSpec((1, TN), lambda i, j: (0, j))],
        out_specs=[pl.BlockSpec((tm, TN), lambda i, j: (i, j)),
                   pl.BlockSpec((tm, LANES), lambda i, j: (i, 0)),
                   pl.BlockSpec((tm, d), lambda i, j: (i, 0))],
        out_shape=[jax.ShapeDtypeStruct((t, Z_TILES * TN), BF16),
                   jax.ShapeDtypeStruct((t, LANES), F32),
                   jax.ShapeDtypeStruct((t, d), BF16)],
        compiler_params=_cparams(("arbitrary", "arbitrary")),
        name="in_projection",
    )(x2d, mod_l, w_in_p, b_in_p, col_scale)


PROJ_PARTS = 2


def _dil_proj_kernel(h_ref, w_ref, b_ref, cs_ref, *refs, tm, dil, rope):
    if rope:
        rc_ref, rp_ref, rm_ref, o_ref, de_scr = refs
    else:
        o_ref, de_scr = refs
    n_part = PROJ_PARTS
    part_rows = tm // n_part
    n_slab = TN // LANES
    accs = []
    for part in range(n_part):
        rows = slice(part * part_rows, (part + 1) * part_rows)
        accs.append((jnp.dot(h_ref[rows, :], w_ref[0], preferred_element_type=F32) + b_ref[0])
                    * cs_ref[...])
    for part, acc in enumerate(accs):
        rows = slice(part * part_rows, (part + 1) * part_rows)
        slabs = [acc[:, s * LANES:(s + 1) * LANES] for s in range(n_slab)]
        if rope:
            rc, rp, rm = rc_ref[rows, :], rp_ref[rows, :], rm_ref[rows, :]
            slabs = [_rope_slab(x, rc, rp, rm, PARTIAL_ROT // 2) for x in slabs]
        if dil == 1:
            o_ref[0, 0, rows, :] = jnp.concatenate(slabs, axis=1).astype(BF16)
        else:
            n = part_rows // dil
            for s in range(n_slab):
                de_scr[part * n_slab + s] = slabs[s]
            for r in range(dil):
                o_ref[0, r, part * n:(part + 1) * n, :] = jnp.concatenate(
                    [de_scr[part * n_slab + s, pl.ds(r, n, stride=dil), :] for s in range(n_slab)],
                    axis=1).astype(BF16)


def dilated_projection(h, w_in_p, b_in_p, col_scale, tabs, g, layer, batch, seq, first_part, n_parts):
    t, d = h.shape
    dil = DILATION_GROUPS[g][1]
    tm = min(1024, seq)
    tiles_per_batch = seq // tm
    tile = lambda p: TILE_B0 + (first_part + p) * N_DIL + g
    col = lambda rows: pl.BlockSpec((1, rows, TN), lambda i, p: (layer, 0, tile(p)))
    in_specs = [pl.BlockSpec((tm, d), lambda i, p: (i, 0)), col(d), col(1),
                pl.BlockSpec((1, TN), lambda i, p: (0, tile(p)))]
    args = [h, w_in_p, b_in_p, col_scale]
    if tabs is not None:
        in_specs += [pl.BlockSpec((tm, LANES), lambda i, p: (i, 0))] * 3
        args += list(tabs)
    return pl.pallas_call(
        functools.partial(_dil_proj_kernel, tm=tm, dil=dil, rope=tabs is not None),
        grid=(t // tm, n_parts),
        in_specs=in_specs,
        out_specs=pl.BlockSpec((1, dil, tm // dil, TN),
                               lambda i, p: (i // tiles_per_batch, 0, i % tiles_per_batch, p)),
        out_shape=jax.ShapeDtypeStruct((batch, dil, seq // dil, n_parts * TN), BF16),
        scratch_shapes=[pltpu.VMEM((PROJ_PARTS * TN // LANES, tm // PROJ_PARTS, LANES), F32)],
        compiler_params=_cparams(("arbitrary", "arbitrary")),
        name="dilated_projection",
    )(*args)


def _in_segments(d):
    o_af = 3 * W_A
    o_b = o_af + A_HEADS
    o_c = o_b + 3 * N_DIL * W_B
    o_gl = o_c + Q_LORA + KV_LORA + D_ROPE
    c_width = o_gl - o_c
    return ((o_gl, 3 * d, 0), (0, o_af, TILE_AQ * TN), (o_c, c_width, TILE_CQ * TN),
            (o_af, A_HEADS, TILE_CKV * TN + AF_LANE0), (o_b, o_c - o_b, TILE_B0 * TN))


def _pack_kernel(w_ref, o_ref, *, segments):
    o_ref[...] = jnp.zeros_like(o_ref)
    for src, width, dst in segments:
        o_ref[0, :, dst:dst + width] = w_ref[0, :, src:src + width].astype(o_ref.dtype)


def pack_in_weights(w_in, out_dtype, rows):
    depth, d_rows, n_in = w_in.shape
    segments = _in_segments((n_in - (3 * W_A + A_HEADS + 3 * N_DIL * W_B + Q_LORA + KV_LORA + D_ROPE))
                            // 3)
    return pl.pallas_call(
        functools.partial(_pack_kernel, segments=segments),
        grid=(depth, d_rows // rows),
        in_specs=[pl.BlockSpec((1, rows, n_in), lambda l, i: (l, i, 0))],
        out_specs=pl.BlockSpec((1, rows, N_TILES * TN), lambda l, i: (l, i, 0)),
        out_shape=jax.ShapeDtypeStruct((depth, d_rows, N_TILES * TN), out_dtype),
        compiler_params=_cparams(("arbitrary", "arbitrary")),
        name="pack_in_weights",
    )(w_in)


def _logf_kernel(zf_ref, lf_ref, carry):
    @pl.when(pl.program_id(1) == 0)
    def _():
        carry[...] = jnp.zeros_like(carry)

    x = zf_ref[...]
    ls = jnp.minimum(x, 0.0) - jnp.log1p(jnp.exp(-jnp.abs(x)))
    ts = x.shape[0]
    row = lax.broadcasted_iota(jnp.int32, (ts, ts), 0)
    col = lax.broadcasted_iota(jnp.int32, (ts, ts), 1)
    tri = (row >= col).astype(F32)
    cs = jnp.dot(tri, ls, precision=HIGHEST, preferred_element_type=F32) + carry[0:1, :]
    lf_ref[...] = cs * LOG2E
    carry[...] = jnp.broadcast_to(cs[ts - 1:ts, :], carry.shape)


def log_forget_cumsum(zf, batch, seq):
    ts = min(256, seq)
    nt = seq // ts
    return pl.pallas_call(
        _logf_kernel,
        grid=(batch, nt),
        in_specs=[pl.BlockSpec((ts, LANES), lambda b, i: (b * nt + i, 0))],
        out_specs=pl.BlockSpec((ts, LANES), lambda b, i: (b * nt + i, 0)),
        out_shape=jax.ShapeDtypeStruct(zf.shape, F32),
        scratch_shapes=[pltpu.VMEM((SUBLANES, LANES), F32)],
        compiler_params=_cparams(("arbitrary", "arbitrary")),
        name="log_forget_cumsum",
    )(zf)


FLASH_TQ = 4096
FLASH_SUB = 512
V_PAD = 16


def _split3(x):
    hi = x.astype(BF16)
    r1 = x - hi.astype(F32)
    mid = r1.astype(BF16)
    lo = (r1 - mid.astype(F32)).astype(BF16)
    return hi, mid, lo


def _flash_kernel(*refs, decay, dv, seq):
    if decay:
        q_ref, k_ref, v_ref, lfq_ref, lfk_ref, o_ref, kext, vext, m_scr, acc_scr = refs
    else:
        q_ref, k_ref, v_ref, o_ref, vext, m_scr, acc_scr = refs
    qi = pl.program_id(2)
    tq = q_ref.shape[0]
    sub = min(FLASH_SUB, tq)
    n_sub = tq // sub
    dn = (((1,), (1,)), ((), ()))

    @pl.when(qi == 0)
    def _():
        lane = lax.broadcasted_iota(jnp.int32, (seq, LANES), 1)
        for c in range(seq // sub):
            cols = slice(c * sub, (c + 1) * sub)
            vext[0:dv, cols] = v_ref[cols, :].astype(F32).T.astype(BF16)
        ones_row = lax.broadcasted_iota(jnp.int32, (V_PAD, seq), 0) == 0
        vext[dv:, :] = jnp.where(ones_row, 1.0, 0.0).astype(BF16)
        if decay:
            hi, mid, lo = _split3(lfk_ref[0, 0])
            kext[:, :HEAD_DIM] = k_ref[...]
            kext[:, HEAD_DIM:] = jnp.where(
                lane == 0, -hi, jnp.where(lane == 1, -mid, jnp.where(
                    lane == 2, -lo, jnp.where(lane < 6, 1.0, 0.0).astype(BF16))))

    if decay:
        lane = lax.broadcasted_iota(jnp.int32, (tq, LANES), 1)
        hi, mid, lo = _split3(lfq_ref[0, 0])
        q_tail = jnp.where(lane == 3, hi, jnp.where(lane == 4, mid, jnp.where(
            lane == 5, lo, jnp.where(lane < 3, 1.0, 0.0).astype(BF16))))
        q_all = jnp.concatenate([q_ref[...], q_tail], axis=1)
        keys = kext
    else:
        q_all = q_ref[...]
        keys = k_ref

    m_scr[...] = jnp.full_like(m_scr, -jnp.inf)
    acc_scr[...] = jnp.zeros_like(acc_scr)

    def update(work):
        cols = [slice(r * sub, (r + 1) * sub) for r, _, _ in work]
        starts = [pl.multiple_of(c * sub, sub) for _, c, _ in work]
        scores = []
        for (r, c, masked), cl, st in zip(work, cols, starts):
            s = lax.dot_general(keys[pl.ds(st, sub), :], q_all[cl], dn, preferred_element_type=F32)
            if masked:
                key = lax.broadcasted_iota(jnp.int32, s.shape, 0)
                qry = lax.broadcasted_iota(jnp.int32, s.shape, 1)
                s = jnp.where(key <= qry, s, -jnp.inf)
            scores.append(s)
        m_prev = [m_scr[0:1, cl] for cl in cols]
        m_new = [jnp.maximum(mp, jnp.max(s, axis=0, keepdims=True)) for mp, s in zip(m_prev, scores)]
        probs = [jnp.exp2(s - mn).astype(BF16) for s, mn in zip(scores, m_new)]
        for cl, st, mp, mn, p in zip(cols, starts, m_prev, m_new, probs):
            acc_scr[:, cl] = (jnp.exp2(mp - mn) * acc_scr[:, cl]
                              + jnp.dot(vext[:, pl.ds(st, sub)], p, preferred_element_type=F32))
            m_scr[:, cl] = jnp.broadcast_to(mn, (SUBLANES, sub))

    def full_chunks(c, carry):
        update([(r, c, False) for r in range(n_sub)])
        return carry

    lax.fori_loop(0, qi * n_sub, full_chunks, 0)
    for d in range(n_sub):
        update([(r, qi * n_sub + d, r == d) for r in range(d, n_sub)])
    for r in range(n_sub):
        cols = slice(r * sub, (r + 1) * sub)
        o_t = acc_scr[0:dv, cols] / acc_scr[dv:dv + 1, cols]
        o_ref[cols, :] = o_t.T.astype(o_ref.dtype)


def flash_attention(q_arr, k_arr, v_arr, q_col0, k_col0, v_col0, dq, dv, heads, batch, seq,
                    decay=None):
    tq = min(FLASH_TQ, seq)
    nq = seq // tq
    in_specs = [
        pl.BlockSpec((tq, dq), lambda b, h, i: (b * nq + i, q_col0 + h)),
        pl.BlockSpec((seq, dq), lambda b, h, i: (b, k_col0 + h)),
        pl.BlockSpec((seq, dv), lambda b, h, i: (b, v_col0 + h)),
    ]
    args = [q_arr, k_arr, v_arr]
    scratch = [pltpu.VMEM((dv + V_PAD, seq), BF16), pltpu.VMEM((SUBLANES, tq), F32),
               pltpu.VMEM((dv + V_PAD, tq), F32)]
    if decay is not None:
        in_specs += [pl.BlockSpec((1, 1, tq, 1), lambda b, h, i: (b, h, i, 0)),
                     pl.BlockSpec((1, 1, seq, 1), lambda b, h, i: (b, h, 0, 0))]
        args += [decay, decay]
        scratch = [pltpu.VMEM((seq, dq + LANES), BF16)] + scratch
    return pl.pallas_call(
        functools.partial(_flash_kernel, decay=decay is not None, dv=dv, seq=seq),
        grid=(batch, heads, nq),
        in_specs=in_specs,
        out_specs=pl.BlockSpec((tq, dv), lambda b, h, i: (b * nq + i, h)),
        out_shape=jax.ShapeDtypeStruct((batch * seq, heads * dv), BF16),
        scratch_shapes=scratch,
        compiler_params=_cparams(("arbitrary", "arbitrary", "arbitrary")),
        name="flash_fox" if decay is not None else "flash_mla",
    )(*args)


DIL_ROWS = 1024


def _dilated_kernel(q_ref, k_ref, v_ref, o_ref, lse_ref, *, rows):
    c = pl.program_id(2)
    wb = WIN_BLOCK
    row = lax.broadcasted_iota(jnp.int32, (wb, wb), 0)
    col = lax.broadcasted_iota(jnp.int32, (wb, wb), 1)
    cur_ok = col <= row
    dn = (((1,), (1,)), ((), ()))
    chains = []
    for n in range(rows // wb):
        base = c * rows + n * wb
        cur = pl.ds(pl.multiple_of(base, wb), wb)
        prev = pl.ds(pl.multiple_of(jnp.maximum(base - wb, 0), wb), wb)
        prev_ok = (col >= row) & (base > 0)
        for h in range(B_HEADS):
            chains.append((slice(n * wb, (n + 1) * wb), slice(h * HEAD_DIM, (h + 1) * HEAD_DIM),
                           cur, prev, prev_ok))
    scores = []
    for qrows, sl, cur, prev, prev_ok in chains:
        q = q_ref[0, 0, qrows, sl]
        s_c = lax.dot_general(q, k_ref[0, 0, cur, sl], dn, preferred_element_type=F32)
        s_p = lax.dot_general(q, k_ref[0, 0, prev, sl], dn, preferred_element_type=F32)
        scores.append((jnp.where(cur_ok, s_c, -jnp.inf), jnp.where(prev_ok, s_p, -jnp.inf)))
    maxes = [jnp.maximum(jnp.max(s_c, axis=1, keepdims=True), jnp.max(s_p, axis=1, keepdims=True))
             for s_c, s_p in scores]
    probs = [(jnp.exp(s_c - m), jnp.exp(s_p - m)) for (s_c, s_p), m in zip(scores, maxes)]
    sums = [jnp.sum(p_c, axis=1, keepdims=True) + jnp.sum(p_p, axis=1, keepdims=True)
            for p_c, p_p in probs]
    for (qrows, sl, cur, prev, _), (p_c, p_p), m, l in zip(chains, probs, maxes, sums):
        o = (jnp.dot(p_c.astype(BF16), v_ref[0, 0, cur, sl], preferred_element_type=F32)
             + jnp.dot(p_p.astype(BF16), v_ref[0, 0, prev, sl], preferred_element_type=F32))
        o_ref[0, 0, qrows, sl] = (o / l).astype(BF16)
        lse_ref[0, 0, qrows, sl] = jnp.broadcast_to(m + jnp.log(l), (wb, HEAD_DIM))


def dilated_attention(zqk, zv, batch):
    _, dil, l_sub, _ = zv.shape
    rows = min(DIL_ROWS, l_sub)
    chunk = lambda c: pl.BlockSpec((1, 1, rows, W_B), lambda b, r, i: (b, r, i, c))
    whole = lambda c: pl.BlockSpec((1, 1, l_sub, W_B), lambda b, r, i: (b, r, 0, c))
    return pl.pallas_call(
        functools.partial(_dilated_kernel, rows=rows),
        grid=(batch, dil, l_sub // rows),
        in_specs=[chunk(0), whole(1), whole(0)],
        out_specs=[chunk(0), chunk(0)],
        out_shape=[jax.ShapeDtypeStruct((batch, dil, l_sub, W_B), BF16),
                   jax.ShapeDtypeStruct((batch, dil, l_sub, W_B), F32)],
        compiler_params=_cparams(("arbitrary", "arbitrary", "arbitrary")),
        name="dilated_attention",
    )(zqk, zqk, zv)


def _rms(x, g):
    return x * lax.rsqrt(jnp.mean(x * x, axis=-1, keepdims=True) + LN_EPS) * g


def _mla_up_kernel(cq_ref, ckv_ref, gq_ref, gkv_ref, wq_ref, wk_ref, wv_ref, rc_ref, rp_ref, rm_ref,
                   q_ref, k_ref, v_ref):
    scale = (D_NOPE + D_ROPE) ** -0.5 * LOG2E
    half = D_ROPE // 2
    rc, rp, rm = rc_ref[...], rp_ref[...], rm_ref[...]
    cqn = _rms(cq_ref[...].astype(F32), gq_ref[...]).astype(BF16)
    q = jnp.dot(cqn, wq_ref[...], preferred_element_type=F32) * scale
    ckvr = ckv_ref[...].astype(F32)
    ckvn = _rms(ckvr[:, :KV_LORA], gkv_ref[...]).astype(BF16)
    k_rope = _rope_slab(ckvr[:, KV_LORA:KV_LORA + LANES], rc, rp, rm, half).astype(BF16)
    k_nope = jnp.dot(ckvn, wk_ref[...], preferred_element_type=F32)
    v_ref[...] = jnp.dot(ckvn, wv_ref[...], preferred_element_type=F32).astype(BF16)
    for h in range(C_HEADS):
        lo = h * 2 * LANES
        q_ref[:, lo:lo + LANES] = q[:, lo:lo + LANES].astype(BF16)
        q_ref[:, lo + LANES:lo + 2 * LANES] = _rope_slab(
            q[:, lo + LANES:lo + 2 * LANES], rc, rp, rm, half).astype(BF16)
        k_ref[:, lo:lo + LANES] = k_nope[:, h * LANES:(h + 1) * LANES].astype(BF16)
        k_ref[:, lo + LANES:lo + 2 * LANES] = k_rope


def mla_up_projection(z, g_cq, g_ckv, wq_p, wk_p, wv_p, tabs_c):
    t = z.shape[0]
    tm = min(512, t)
    rc, rp, rm = tabs_c
    full = lambda a: pl.BlockSpec(a.shape, lambda i: (0,) * a.ndim)
    tab = pl.BlockSpec((tm, LANES), lambda i: (i, 0))
    row = lambda w: pl.BlockSpec((tm, w), lambda i: (i, 0))
    return pl.pallas_call(
        _mla_up_kernel,
        grid=(t // tm,),
        in_specs=[pl.BlockSpec((tm, TN), lambda i: (i, TILE_CQ)),
                  pl.BlockSpec((tm, TN), lambda i: (i, TILE_CKV)),
                  full(g_cq), full(g_ckv), full(wq_p), full(wk_p), full(wv_p), tab, tab, tab],
        out_specs=[row(C_HEADS * 2 * LANES), row(C_HEADS * 2 * LANES), row(W_C)],
        out_shape=[jax.ShapeDtypeStruct((t, C_HEADS * 2 * LANES), BF16),
                   jax.ShapeDtypeStruct((t, C_HEADS * 2 * LANES), BF16),
                   jax.ShapeDtypeStruct((t, W_C), BF16)],
        compiler_params=_cparams(("arbitrary",)),
        name="mla_up_projection",
    )(z, z, g_cq, g_ckv, wq_p, wk_p, wv_p, rc, rp, rm)


def pack_mla_weights(w_uq_l, w_ukv_l):
    wq = w_uq_l.reshape(Q_LORA, C_HEADS, D_NOPE + D_ROPE)
    wq = jnp.pad(wq, ((0, 0), (0, 0), (0, 2 * LANES - D_NOPE - D_ROPE)))
    wkv = w_ukv_l.reshape(KV_LORA, C_HEADS, D_NOPE + D_VC)
    wk = wkv[:, :, :D_NOPE].reshape(KV_LORA, C_HEADS * D_NOPE)
    wv = wkv[:, :, D_NOPE:].reshape(KV_LORA, W_C)
    return (wq.reshape(Q_LORA, C_HEADS * 2 * LANES).astype(BF16), wk.astype(BF16), wv.astype(BF16))


def _merge_kernel(gla_ref, glb_ref, glc_ref, oa_ref, ob0_ref, ob1_ref, ob2_ref,
                  ls0_ref, ls1_ref, ls2_ref, oc_ref, wa_ref, wb_ref, wc_ref, y_ref,
                  o_scr, l_scr, *, tm):
    for g, (o_ref, ls_ref) in enumerate(((ob0_ref, ls0_ref), (ob1_ref, ls1_ref), (ob2_ref, ls2_ref))):
        dil = DILATION_GROUPS[g][1]
        for h in range(B_HEADS):
            sl = slice(h * HEAD_DIM, (h + 1) * HEAD_DIM)
            for r in range(dil):
                rows = pl.ds(r, tm // dil, stride=dil) if dil > 1 else slice(None)
                o_scr[g * B_HEADS + h, rows, :] = o_ref[0, r, :, sl].astype(F32)
                l_scr[g * B_HEADS + h, rows, :] = ls_ref[0, r, :, sl]
    slabs = []
    for h in range(B_HEADS):
        l0, l1, l2 = l_scr[h], l_scr[B_HEADS + h], l_scr[2 * B_HEADS + h]
        m = jnp.maximum(jnp.maximum(l0, l1), l2)
        e0, e1, e2 = jnp.exp(l0 - m), jnp.exp(l1 - m), jnp.exp(l2 - m)
        den = e0 + e1 + e2
        slabs.append((e0 / den) * o_scr[h] + (e1 / den) * o_scr[B_HEADS + h]
                     + (e2 / den) * o_scr[2 * B_HEADS + h])
    o_b = jnp.concatenate(slabs, axis=1).astype(BF16)
    pa = jnp.dot(oa_ref[...], wa_ref[...], preferred_element_type=F32)
    pb = jnp.dot(o_b, wb_ref[...], preferred_element_type=F32)
    pc = jnp.dot(oc_ref[...], wc_ref[...], preferred_element_type=F32)
    y = (jax.nn.sigmoid(gla_ref[...].astype(F32)) * pa
         + jax.nn.sigmoid(glb_ref[...].astype(F32)) * pb
         + jax.nn.sigmoid(glc_ref[...].astype(F32)) * pc)
    y_ref[...] = y.astype(BF16)


def branch_merge(z, o_a, obs, lses, o_c, wa, wb, wc, batch, seq):
    t = z.shape[0]
    d = wa.shape[1]
    tm = min(512, seq)
    tpb = seq // tm
    full = lambda a: pl.BlockSpec(a.shape, lambda i: (0,) * a.ndim)
    gl = lambda c: pl.BlockSpec((tm, d), lambda i: (i, c))
    res = lambda dil: pl.BlockSpec((1, dil, tm // dil, W_B), lambda i: (i // tpb, 0, i % tpb, 0))
    res_specs = [res(dil) for _, dil in DILATION_GROUPS]
    return pl.pallas_call(
        functools.partial(_merge_kernel, tm=tm),
        grid=(t // tm,),
        in_specs=[gl(0), gl(1), gl(2), pl.BlockSpec((tm, W_A), lambda i: (i, 0))]
        + res_specs + res_specs
        + [pl.BlockSpec((tm, W_C), lambda i: (i, 0)), full(wa), full(wb), full(wc)],
        out_specs=pl.BlockSpec((tm, d), lambda i: (i, 0)),
        out_shape=jax.ShapeDtypeStruct((t, d), BF16),
        scratch_shapes=[pltpu.VMEM((N_DIL * B_HEADS, tm, HEAD_DIM), F32),
                        pltpu.VMEM((N_DIL * B_HEADS, tm, HEAD_DIM), F32)],
        compiler_params=_cparams(("arbitrary",)),
        name="branch_merge",
    )(z, z, z, o_a, *obs, *lses, o_c, wa, wb, wc)


def _layer_norm(x, g, b):
    mu = jnp.mean(x, axis=-1, keepdims=True)
    xc = x - mu
    var = jnp.mean(xc * xc, axis=-1, keepdims=True)
    return xc * lax.rsqrt(var + LN_EPS) * g + b


def _out_router_kernel(y_ref, x_ref, mod_ref, wo_ref, lng_ref, lnb_ref, wr_ref, br_ref,
                       x1_ref, h2_ref, ri_ref, rf_ref, cnt_ref, carry, *, tm, alpha):
    @pl.when(pl.program_id(0) == 0)
    def _():
        carry[...] = jnp.zeros_like(carry)

    gate1 = mod_ref[0, 2:3, :]
    shift2 = mod_ref[0, 3:4, :]
    scale2 = mod_ref[0, 4:5, :]
    mix = jnp.dot(y_ref[...], wo_ref[...], preferred_element_type=F32)
    x1 = _layer_norm(alpha * x_ref[...] + (1.0 + gate1) * mix, lng_ref[...], lnb_ref[...])
    x1_ref[...] = x1
    h2 = x1 * (1.0 + scale2) + shift2
    d = h2.shape[1]
    for c in range(d // LANES):
        h2_ref[pl.ds(c, tm, stride=d // LANES), :] = h2[:, c * LANES:(c + 1) * LANES]

    h_hi = h2.astype(BF16)
    h_lo = (h2 - h_hi.astype(F32)).astype(BF16)
    logits = (jnp.dot(h_hi, wr_ref[0], preferred_element_type=F32)
              + jnp.dot(h_lo, wr_ref[0], preferred_element_type=F32)
              + jnp.dot(h_hi, wr_ref[1], preferred_element_type=F32))
    s = jax.nn.sigmoid(logits.T[:N_EXPERTS])
    sel = s + br_ref[...]
    sel_m = [sel[i * N_GROUPS:(i + 1) * N_GROUPS] for i in range(EXPERTS_PER_GROUP)]
    s_m = [s[i * N_GROUPS:(i + 1) * N_GROUPS] for i in range(EXPERTS_PER_GROUP)]
    gscore = None
    for a in range(EXPERTS_PER_GROUP):
        for b in range(a + 1, EXPERTS_PER_GROUP):
            pair = sel_m[a] + sel_m[b]
            gscore = pair if gscore is None else jnp.maximum(gscore, pair)
    grp = lax.broadcasted_iota(jnp.int32, (N_GROUPS, tm), 0).astype(F32)
    gmax = jnp.max(gscore, axis=0, keepdims=True)
    gbest = jnp.min(jnp.where(gscore == gmax, grp, float(N_GROUPS)), axis=0, keepdims=True)
    in_best = grp == gbest
    v = [jnp.sum(jnp.where(in_best, m, 0.0), axis=0, keepdims=True) for m in sel_m]
    u = [jnp.sum(jnp.where(in_best, m, 0.0), axis=0, keepdims=True) for m in s_m]

    def first_argmax(vals):
        best = vals[0]
        for x in vals[1:]:
            best = jnp.maximum(best, x)
        idx = jnp.full_like(gbest, float(EXPERTS_PER_GROUP - 1))
        for i in range(EXPERTS_PER_GROUP - 2, -1, -1):
            idx = jnp.where(vals[i] == best, float(i), idx)
        return idx

    i1 = first_argmax(v)
    i2 = first_argmax([jnp.where(i1 == i, -jnp.inf, v[i]) for i in range(EXPERTS_PER_GROUP)])

    def member(vals, idx):
        out = vals[EXPERTS_PER_GROUP - 1]
        for i in range(EXPERTS_PER_GROUP - 2, -1, -1):
            out = jnp.where(idx == i, vals[i], out)
        return out

    u1, u2 = member(u, i1), member(u, i2)
    w1 = u1 / (u1 + u2)
    w2 = u2 / (u1 + u2)
    e1 = gbest * EXPERTS_PER_GROUP + i1
    e2 = gbest * EXPERTS_PER_GROUP + i2
    row1 = i1 * N_GROUPS + gbest
    row2 = i2 * N_GROUPS + gbest

    erow = lax.broadcasted_iota(jnp.int32, (N_EXPERTS, tm), 0).astype(F32)
    hit1, hit2 = erow == row1, erow == row2
    onehot = (hit1 | hit2).astype(F32)
    t_row = lax.broadcasted_iota(jnp.int32, (tm, tm), 0)
    t_col = lax.broadcasted_iota(jnp.int32, (tm, tm), 1)
    earlier = (t_row < t_col).astype(BF16)
    cnt = jnp.dot(onehot.astype(BF16), earlier, preferred_element_type=F32) + carry[:, 0:1]
    r1 = jnp.sum(jnp.where(hit1, cnt, 0.0), axis=0, keepdims=True)
    r2 = jnp.sum(jnp.where(hit2, cnt, 0.0), axis=0, keepdims=True)
    new_carry = carry[...] + jnp.sum(onehot, axis=1, keepdims=True)
    carry[...] = new_carry
    cnt_ref[...] = new_carry

    sub = lax.broadcasted_iota(jnp.int32, (SUBLANES, tm), 0)
    ri = jnp.where(sub == 0, e1, jnp.where(sub == 1, e2, jnp.where(sub == 2, r1, jnp.where(
        sub == 3, r2, 0.0))))
    ri_ref[0] = ri.astype(jnp.int32)
    rf_ref[0] = jnp.where(sub == 0, w1, jnp.where(sub == 1, w2, 0.0))


def out_proj_router(y, x2d, mod_l, w_o, ln_g, ln_b, wr_p, br_p, batch, seq, alpha):
    t, d = x2d.shape
    tm = min(256, seq)
    tpb = seq // tm
    nt = t // tm
    full = lambda a: pl.BlockSpec(a.shape, lambda i: (0,) * a.ndim)
    tile = pl.BlockSpec((tm, d), lambda i: (i, 0))
    rows = pl.BlockSpec((1, SUBLANES, tm), lambda i: (i, 0, 0))
    rt = d // LANES
    return pl.pallas_call(
        functools.partial(_out_router_kernel, tm=tm, alpha=alpha),
        grid=(nt,),
        in_specs=[tile, tile, pl.BlockSpec((1, 6, d), lambda i: (i // tpb, 0, 0)),
                  full(w_o), full(ln_g), full(ln_b), full(wr_p), full(br_p)],
        out_specs=[tile, pl.BlockSpec((tm * rt, LANES), lambda i: (i, 0)), rows, rows,
                   pl.BlockSpec((N_EXPERTS, LANES), lambda i: (0, 0))],
        out_shape=[jax.ShapeDtypeStruct((t, d), F32),
                   jax.ShapeDtypeStruct((t * rt, LANES), F32),
                   jax.ShapeDtypeStruct((nt, SUBLANES, tm), jnp.int32),
                   jax.ShapeDtypeStruct((nt, SUBLANES, tm), F32),
                   jax.ShapeDtypeStruct((N_EXPERTS, LANES), F32)],
        scratch_shapes=[pltpu.VMEM((N_EXPERTS, LANES), F32)],
        compiler_params=_cparams(("arbitrary",)),
        name="out_proj_router",
    )(y, x2d, mod_l, w_o, ln_g, ln_b, wr_p, br_p)


def _dispatch_kernel(dest_ref, h_ref, xs_in_ref, xs_ref, sem, *, tm, rt):
    del xs_in_ref

    def row_copy(a, t):
        return pltpu.make_async_copy(h_ref.at[pl.ds(t * rt, rt), :], xs_ref.at[dest_ref[0, 0, a]], sem)

    def issue(t, carry):
        row_copy(t, t).start(priority=0)
        row_copy(tm + t, t).start(priority=1)
        return carry

    def drain(t, carry):
        row_copy(t, t).wait()
        row_copy(tm + t, t).wait()
        return carry

    lax.fori_loop(0, tm, issue, 0, unroll=8)
    lax.fori_loop(0, tm, drain, 0, unroll=8)


def moe_dispatch(dest_tiles, h2t, xs_buf, tm):
    nt = dest_tiles.shape[0]
    rt = h2t.shape[0] // (nt * tm)
    return pl.pallas_call(
        functools.partial(_dispatch_kernel, tm=tm, rt=rt),
        grid=(nt,),
        in_specs=[pl.BlockSpec((1, 1, 2 * tm), lambda i: (i, 0, 0), memory_space=pltpu.SMEM),
                  pl.BlockSpec((tm * rt, LANES), lambda i: (i, 0)),
                  pl.BlockSpec(memory_space=pl.ANY)],
        out_specs=pl.BlockSpec(memory_space=pl.ANY),
        out_shape=jax.ShapeDtypeStruct(xs_buf.shape, xs_buf.dtype),
        scratch_shapes=[pltpu.SemaphoreType.DMA(())],
        input_output_aliases={2: 0},
        compiler_params=_cparams(("arbitrary",)),
        name="moe_dispatch",
    )(dest_tiles, h2t, xs_buf)


def _expert_kernel(be_ref, nu_ref, xs_ref, wg_ref, wu_ref, wd_ref, ys_ref, wg_s, wu_s, wd_s, *, rt):
    n = pl.program_id(0)

    @pl.when(n < nu_ref[0])
    def _():
        prev = be_ref[jnp.maximum(n - 1, 0)]

        @pl.when((n == 0) | (be_ref[n] != prev))
        def _():
            wg_s[...] = wg_ref[0, 0].astype(BF16)
            wu_s[...] = wu_ref[0, 0].astype(BF16)
            wd_s[...] = wd_ref[0, 0].astype(BF16)

        pr = MOE_BLOCK // MOE_PARTS

        def ffn(parts):
            xs = [jnp.concatenate([xs_ref[pl.ds(p * pr * rt + c, pr, stride=rt), :]
                                   for c in range(rt)], axis=1).astype(BF16) for p in parts]
            gates = [jnp.dot(x, wg_s[...], preferred_element_type=F32) for x in xs]
            ups = [jnp.dot(x, wu_s[...], preferred_element_type=F32) for x in xs]
            acts = [(g * jax.nn.sigmoid(g) * u).astype(BF16) for g, u in zip(gates, ups)]
            ys = [jnp.dot(a, wd_s[...], preferred_element_type=F32) for a in acts]
            for p, y in zip(parts, ys):
                for c in range(rt):
                    ys_ref[pl.ds(p * pr * rt + c, pr, stride=rt), :] = y[:, c * LANES:(c + 1) * LANES]

        lead = (MOE_PARTS - 1) * pr

        @pl.when(nu_ref[1 + n] > lead)
        def _():
            ffn(list(range(MOE_PARTS)))

        @pl.when(nu_ref[1 + n] <= lead)
        def _():
            ffn(list(range(MOE_PARTS - 1)))
            ys_ref[lead * rt:, :] = jnp.zeros((pr * rt, LANES), F32)

    @pl.when(n >= nu_ref[0])
    def _():
        ys_ref[...] = jnp.zeros_like(ys_ref)


def expert_ffn(blk_e, n_used, xs2d, w_gate, w_up, w_down, layer):
    depth, n_exp, d, de = w_gate.shape
    rt = d // LANES
    n_blk = xs2d.shape[0] // (MOE_BLOCK * rt)
    rows = lambda n, be, nu: (jnp.minimum(n, nu[0] - 1), 0)
    grid_spec = pltpu.PrefetchScalarGridSpec(
        num_scalar_prefetch=2,
        grid=(n_blk,),
        in_specs=[pl.BlockSpec((MOE_BLOCK * rt, LANES), rows),
                  pl.BlockSpec((1, 1, d, de), lambda n, be, nu: (layer, be[n], 0, 0)),
                  pl.BlockSpec((1, 1, d, de), lambda n, be, nu: (layer, be[n], 0, 0)),
                  pl.BlockSpec((1, 1, de, d), lambda n, be, nu: (layer, be[n], 0, 0))],
        out_specs=pl.BlockSpec((MOE_BLOCK * rt, LANES), lambda n, be, nu: (n, 0)),
        scratch_shapes=[pltpu.VMEM((d, de), BF16), pltpu.VMEM((d, de), BF16),
                        pltpu.VMEM((de, d), BF16)])
    return pl.pallas_call(
        functools.partial(_expert_kernel, rt=rt),
        grid_spec=grid_spec,
        out_shape=jax.ShapeDtypeStruct(xs2d.shape, F32),
        compiler_params=_cparams(("arbitrary",)),
        name="expert_ffn",
    )(blk_e, n_used, xs2d, w_gate, w_up, w_down)


def _combine_kernel(dest_ref, dnext_ref, ys_ref, rf_ref, x_ref, mod_ref, lng_ref, lnb_ref, o_ref,
                    buf, sem, *, tm, rt, alpha, nt):
    i = pl.program_id(0)
    slot = i % 2

    def row_copy(dref, a, sl):
        return pltpu.make_async_copy(ys_ref.at[dref[0, 0, a]], buf.at[sl, pl.ds(a * rt, rt), :],
                                     sem.at[sl])

    def issue(dref, sl):
        def body(a, carry):
            row_copy(dref, a, sl).start(priority=0)
            row_copy(dref, tm + a, sl).start(priority=1)
            return carry
        lax.fori_loop(0, tm, body, 0, unroll=8)

    @pl.when(i == 0)
    def _():
        issue(dest_ref, 0)

    @pl.when(i + 1 < nt)
    def _():
        issue(dnext_ref, 1 - slot)

    def drain(a, carry):
        row_copy(dest_ref, a, slot).wait()
        return carry

    lax.fori_loop(0, 2 * tm, drain, 0, unroll=8)
    cur = buf.at[slot]

    def rows(k):
        return jnp.concatenate(
            [cur[pl.ds(k * tm * rt + c, tm, stride=rt), :] for c in range(rt)], axis=1)

    ff = rf_ref[:, 0:1] * rows(0) + rf_ref[:, 1:2] * rows(1)
    gate2 = mod_ref[0, 5:6, :]
    o_ref[...] = _layer_norm(alpha * x_ref[...] + (1.0 + gate2) * ff, lng_ref[...], lnb_ref[...])


def moe_combine(dest_tiles, ys3d, rf, x1, mod_l, ln_g, ln_b, batch, seq, tm, alpha):
    t, d = x1.shape
    rt = d // LANES
    tpb = seq // tm
    full = lambda a: pl.BlockSpec(a.shape, lambda i: (0,) * a.ndim)
    tile = pl.BlockSpec((tm, d), lambda i: (i, 0))
    nt = t // tm
    return pl.pallas_call(
        functools.partial(_combine_kernel, tm=tm, rt=rt, alpha=alpha, nt=nt),
        grid=(nt,),
        in_specs=[pl.BlockSpec((1, 1, 2 * tm), lambda i: (i, 0, 0), memory_space=pltpu.SMEM),
                  pl.BlockSpec((1, 1, 2 * tm), lambda i: (jnp.minimum(i + 1, nt - 1), 0, 0),
                               memory_space=pltpu.SMEM),
                  pl.BlockSpec(memory_space=pl.ANY),
                  pl.BlockSpec((tm, TOP_K), lambda i: (i, 0)),
                  tile, pl.BlockSpec((1, 6, d), lambda i: (i // tpb, 0, 0)), full(ln_g), full(ln_b)],
        out_specs=tile,
        out_shape=jax.ShapeDtypeStruct((t, d), F32),
        scratch_shapes=[pltpu.VMEM((2, 2 * tm * rt, LANES), F32), pltpu.SemaphoreType.DMA((2,))],
        compiler_params=_cparams(("arbitrary",)),
        name="moe_combine",
    )(dest_tiles, dest_tiles, ys3d, rf, x1, mod_l, ln_g, ln_b)


def _routing_tables(ri, cnt, n_rows, tm):
    nt = ri.shape[0]
    counts = cnt[:, 0].astype(jnp.int32)
    counts = counts.reshape(EXPERTS_PER_GROUP, N_GROUPS).T.reshape(N_EXPERTS)
    padded = (counts + MOE_BLOCK - 1) // MOE_BLOCK * MOE_BLOCK
    pend = jnp.cumsum(padded)
    pstart = pend - padded
    e_sel = ri[:, 0:TOP_K, :, None] == jnp.arange(N_EXPERTS, dtype=jnp.int32)
    dest = jnp.sum(jnp.where(e_sel, pstart, 0), axis=-1) + ri[:, TOP_K:2 * TOP_K, :]
    dest_tiles = dest.reshape(nt, 1, TOP_K * tm)
    n_blk = n_rows // MOE_BLOCK
    blk_start = jnp.arange(n_blk, dtype=jnp.int32) * MOE_BLOCK
    blk_e = jnp.minimum(jnp.sum(pend[None, :] <= blk_start[:, None], axis=1), N_EXPERTS - 1)
    blk_sel = blk_e[:, None] == jnp.arange(N_EXPERTS, dtype=jnp.int32)
    seg_end = jnp.sum(jnp.where(blk_sel, pstart + counts, 0), axis=1)
    blk_valid = jnp.clip(seg_end - blk_start, 0, MOE_BLOCK)
    n_used = jnp.concatenate([pend[-1:] // MOE_BLOCK, blk_valid]).astype(jnp.int32)
    return dest_tiles.astype(jnp.int32), blk_e.astype(jnp.int32), n_used


def kernel(x, c, positions, w_ada, b_ada, w_in, b_in, g_cq, g_ckv, w_uq, w_ukv, w_pa, w_pb, w_pc, w_o,
           ln1_g, ln1_b, w_router, b_router, w_gate, w_up, w_down, ln2_g, ln2_b):
    batch, seq, d = x.shape
    depth = w_ada.shape[0]
    t = batch * seq
    alpha = (2 * depth) ** 0.25
    rt = d // LANES
    assert rt == ROW_TILE and seq % 2048 == 0 and d % TN == 0
    tm_moe = min(256, seq)

    c_pad = jnp.zeros((SUBLANES, d), F32).at[:batch].set(c)
    mod_all = ada_modulation(c_pad, w_ada, b_ada)[:, :batch].reshape(depth, batch, 6, d)
    pos_col = positions.reshape(t, 1)
    tabs_b = rope_tables(pos_col, PARTIAL_ROT)
    tabs_c = rope_tables(pos_col, D_ROPE)
    qk_scale = HEAD_DIM ** -0.5
    col_scale = jnp.ones((1, N_TILES * TN), F32)
    col_scale = col_scale.at[:, TILE_AQ * TN:(TILE_AQ + 1) * TN].set(qk_scale * LOG2E)
    col_scale = col_scale.at[:, TILE_B0 * TN:(TILE_B0 + N_DIL) * TN].set(qk_scale)

    perm = np.arange(N_EXPERTS).reshape(N_GROUPS, EXPERTS_PER_GROUP).T.reshape(-1)
    wr_f = jnp.zeros((d, LANES), F32).at[:, :N_EXPERTS].set(w_router.astype(F32)[:, perm])
    wr_hi = wr_f.astype(BF16)
    wr_p = jnp.stack([wr_hi, (wr_f - wr_hi.astype(F32)).astype(BF16)])
    br_p = b_router.astype(F32)[perm][:, None]

    n_rows = -(-(t * TOP_K + N_EXPERTS * (MOE_BLOCK - 1)) // MOE_BLOCK) * MOE_BLOCK
    xs_buf = jnp.zeros((n_rows, rt, LANES), F32)

    w_in_p = pack_in_weights(w_in, BF16, rows=128)
    b_in_p = pack_in_weights(b_in[:, None, :], F32, rows=1)

    x2d = x.reshape(t, d)
    for l in range(depth):
        z, zf, h = in_projection(x2d, mod_all[l], w_in_p, b_in_p, col_scale, l, batch, seq)
        zqks = [dilated_projection(h, w_in_p, b_in_p, col_scale, tabs_b, g, l, batch, seq, 0, 2)
                for g in range(N_DIL)]
        zvs = [dilated_projection(h, w_in_p, b_in_p, col_scale, None, g, l, batch, seq, 2, 1)
               for g in range(N_DIL)]

        lf = log_forget_cumsum(zf, batch, seq)[:, :A_HEADS].reshape(batch, seq, A_HEADS)
        lf = lf.transpose(0, 2, 1)[:, :, :, None]
        o_a = flash_attention(z, z, z, TILE_AQ * 4, TILE_AQ * 4 + 4, TILE_AQ * 4 + 8, HEAD_DIM,
                              HEAD_DIM, A_HEADS, batch, seq, decay=lf)

        obs, lses = [], []
        for zqk, zv in zip(zqks, zvs):
            o_g, lse_g = dilated_attention(zqk, zv, batch)
            obs.append(o_g)
            lses.append(lse_g)

        wq_p, wk_p, wv_p = pack_mla_weights(w_uq[l], w_ukv[l])
        q_c, k_c, v_c = mla_up_projection(z, g_cq[l][None, :], g_ckv[l][None, :], wq_p, wk_p, wv_p,
                                          tabs_c)
        o_c = flash_attention(q_c, k_c, v_c, 0, 0, 0, 2 * LANES, D_VC, C_HEADS, batch, seq)

        y = branch_merge(z, o_a, obs, lses, o_c, w_pa[l].astype(BF16), w_pb[l].astype(BF16),
                         w_pc[l].astype(BF16), batch, seq)
        x1, h2t, ri, rf, cnt = out_proj_router(y, x2d, mod_all[l], w_o[l].astype(BF16),
                                               ln1_g[l][None, :], ln1_b[l][None, :], wr_p, br_p,
                                               batch, seq, alpha)

        dest_tiles, blk_e, n_used = _routing_tables(ri, cnt, n_rows, tm_moe)
        wts = rf[:, 0:TOP_K, :].transpose(0, 2, 1).reshape(t, TOP_K)
        xs_buf = moe_dispatch(dest_tiles, h2t, xs_buf, tm_moe)
        ys = expert_ffn(blk_e, n_used, xs_buf.reshape(n_rows * rt, LANES), w_gate, w_up, w_down, l)
        x2d = moe_combine(dest_tiles, ys.reshape(n_rows, rt, LANES), wts, x1, mod_all[l],
                          ln2_g[l][None, :], ln2_b[l][None, :], batch, seq, tm_moe, alpha)
    return x2d.reshape(batch, seq, d)
```
